```python
import math
import jax, jax.numpy as jnp
from jax import lax
import numpy as np

D_MODEL = 1024
BATCH = 8
SEQ = 2048
DEPTH = 4

GRID_W = 64
CTX_LEN = 256
N_MOD = 9
D_FF = int(math.ceil(8 * D_MODEL / 3 / 128)) * 128
FFN_RES = 0.5
ROPE_BASE = 10000.0
EPS = 1e-6
NEG_INF = -1e30

A_WIDTH = D_MODEL // 4
A_HEADS = 4
A_HEAD_DIM = A_WIDTH // A_HEADS
CHUNK = 128
B_WIDTH = D_MODEL // 2
B_HEADS = 4
B_V_DIM = B_WIDTH // B_HEADS
B_QK_DIM = B_V_DIM // 2
C_WIDTH = D_MODEL // 4
C_HEAD_DIM = 64
C_HEADS = C_WIDTH // C_HEAD_DIM
C_KV_HEADS = 2
C_GROUP = C_HEADS // C_KV_HEADS
WINDOW = 128
BLK = 128
D_MIX = A_WIDTH + B_WIDTH + C_WIDTH

W_UV_A = 2 * A_WIDTH
W_Q_B = B_HEADS * 2 * B_QK_DIM
W_Q_C = C_HEADS * C_HEAD_DIM
W_K_B = B_HEADS * 2 * B_QK_DIM
W_V_B = B_HEADS * B_V_DIM
W_K_C = C_KV_HEADS * C_HEAD_DIM
W_V_C = C_KV_HEADS * C_HEAD_DIM
IN_COLS = W_UV_A + W_Q_B + W_Q_C + W_K_B + W_V_B + W_K_C + W_V_C
KV_START = W_UV_A + W_Q_B + W_Q_C
SPLITS = (W_UV_A, W_UV_A + W_Q_B, KV_START, KV_START + W_K_B, KV_START + W_K_B + W_V_B,
          KV_START + W_K_B + W_V_B + W_K_C)
KV_SPLITS = (W_K_B, W_K_B + W_V_B, W_K_B + W_V_B + W_K_C)

kernel_name = 'hybrid_diffusion_trunk'


def rmsnorm(x, g):
    xf = x.astype(jnp.float32)
    y = xf * lax.rsqrt(jnp.mean(xf * xf, axis=-1, keepdims=True) + EPS)
    return (y * g.astype(jnp.float32)).astype(x.dtype)


def axial_rope_tables(rows, dim):
    row = jnp.repeat(jnp.arange(rows, dtype=jnp.float32), GRID_W)
    col = jnp.tile(jnp.arange(GRID_W, dtype=jnp.float32), rows)
    quarter = dim // 4
    inv = ROPE_BASE ** (-jnp.arange(quarter, dtype=jnp.float32) / quarter)
    ar = row[:, None] * inv[None, :]
    ac = col[:, None] * inv[None, :]
    ang = jnp.concatenate([ar, ar, ac, ac], axis=-1)
    return jnp.cos(ang), jnp.sin(ang)


def apply_rope(x, cos, sin):
    shape = (x.shape[1],) + (1,) * (x.ndim - 3) + (x.shape[-1],)
    cos = cos.reshape(shape).astype(x.dtype)
    sin = sin.reshape(shape).astype(x.dtype)
    x1, x2, x3, x4 = jnp.split(x, 4, axis=-1)
    rot = jnp.concatenate([-x2, x1, -x4, x3], axis=-1)
    return x * cos + rot * sin


def adaln_in(s, g_pre, shift, scale):
    return rmsnorm(s, g_pre) * (1 + scale) + shift


def adaln_out(s, y, g_post, gate, weight):
    return s + weight * gate * rmsnorm(y, g_post)


def swiglu(y, wg, wu, wd):
    return (jax.nn.silu(y @ wg) * (y @ wu)) @ wd


def ffn_sublayer(s, mod, j, g_pre, g_post, wg, wu, wd):
    y = adaln_in(s, g_pre, mod[..., 3 * j, :, :], mod[..., 3 * j + 1, :, :])
    return adaln_out(s, swiglu(y, wg, wu, wd), g_post, mod[..., 3 * j + 2, :, :], FFN_RES)


def chunk_gmlp(uv, v_gain, w_s, b_s):
    b_, l_, _ = uv.shape
    u, v = jnp.split(jax.nn.gelu(uv), 2, axis=-1)
    v = rmsnorm(v.reshape(b_, l_, A_HEADS, A_HEAD_DIM), v_gain)
    v = v.reshape(b_, l_ // CHUNK, CHUNK, A_HEADS, A_HEAD_DIM)
    mixed = jnp.einsum('hpq,bnqhc->bnphc', w_s, v) + b_s.T[:, :, None]
    return u * mixed.reshape(b_, l_, A_WIDTH)


def diff_softmax(q, k, v, lam):
    s = jnp.einsum('bqhmd,bkhmd->bhmqk', q, k).astype(jnp.float32) * (B_QK_DIM ** -0.5)
    p = jax.nn.softmax(s, axis=-1)
    w = p[:, :, 0] - lam * p[:, :, 1]
    return jnp.einsum('bhqk,bkhe->bqhe', w.astype(v.dtype), v)


def diff_attention_latent(q, k_all, v_all, lam):
    b_, l_ = q.shape[:2]
    nb = l_ // BLK
    qb = jnp.moveaxis(q.reshape((b_, nb, BLK) + q.shape[2:]), 1, 0)
    ob = lax.map(lambda qi: diff_softmax(qi, k_all, v_all, lam), qb)
    return jnp.moveaxis(ob, 0, 1).reshape((b_, l_) + ob.shape[3:])


def diff_post(o, g_sub, lam_init):
    return (rmsnorm(o, g_sub) * (1 - lam_init)).reshape(o.shape[0], o.shape[1], -1)


def swa_latent(q, k, v, kc, vc, sink):
    b_, l_ = q.shape[:2]
    nb = l_ // BLK
    qb = q.reshape(b_, nb, BLK, C_KV_HEADS, C_GROUP, C_HEAD_DIM)
    pad = ((0, 0), (1, 1), (0, 0), (0, 0), (0, 0))
    kp = jnp.pad(k.reshape(b_, nb, BLK, C_KV_HEADS, C_HEAD_DIM), pad)
    vp = jnp.pad(v.reshape(b_, nb, BLK, C_KV_HEADS, C_HEAD_DIM), pad)
    kband = jnp.concatenate([kp[:, :-2], kp[:, 1:-1], kp[:, 2:]], axis=2)
    vband = jnp.concatenate([vp[:, :-2], vp[:, 1:-1], vp[:, 2:]], axis=2)
    scale = C_HEAD_DIM ** -0.5
    s_band = jnp.einsum('bnqkgd,bnjkd->bnkgqj', qb, kband).astype(jnp.float32) * scale
    blk = jnp.arange(nb)
    qpos = blk[:, None] * BLK + jnp.arange(BLK)[None, :]
    kpos = (blk[:, None] - 1) * BLK + jnp.arange(3 * BLK)[None, :]
    valid = ((jnp.abs(qpos[:, :, None] - kpos[:, None, :]) <= WINDOW)
             & (kpos[:, None, :] >= 0) & (kpos[:, None, :] < l_))
    s_band = jnp.where(valid[None, :, None, None], s_band, NEG_INF)
    s_ctx = jnp.einsum('bnqkgd,bckd->bnkgqc', qb, kc).astype(jnp.float32) * scale
    s_sink = jnp.broadcast_to(sink.astype(jnp.float32)[None, None, :, :, None, None],
                              s_band.shape[:-1] + (1,))
    p = jax.nn.softmax(jnp.concatenate([s_band, s_ctx, s_sink], axis=-1), axis=-1)
    p_band = p[..., :3 * BLK].astype(v.dtype)
    p_ctx = p[..., 3 * BLK:-1].astype(v.dtype)
    out = (jnp.einsum('bnkgqj,bnjkd->bnqkgd', p_band, vband)
           + jnp.einsum('bnkgqc,bckd->bnqkgd', p_ctx, vc))
    return out.reshape(b_, l_, C_WIDTH)


def swa_context(q, kc, vc, sink):
    b_, l_ = q.shape[:2]
    s = jnp.einsum('bqkgd,bckd->bkgqc', q, kc).astype(jnp.float32) * (C_HEAD_DIM ** -0.5)
    s_sink = jnp.broadcast_to(sink.astype(jnp.float32)[None, :, :, None, None], s.shape[:-1] + (1,))
    p = jax.nn.softmax(jnp.concatenate([s, s_sink], axis=-1), axis=-1)[..., :-1]
    out = jnp.einsum('bkgqc,bckd->bqkgd', p.astype(vc.dtype), vc)
    return out.reshape(b_, l_, C_WIDTH)


def setup_inputs(seed: int = 0) -> dict:
    key = jax.random.key(seed)
    ks = jax.random.split(key, 19)
    f32 = jnp.float32
    nrm = lambda k, shape, s: jax.random.normal(k, shape, f32) * s
    return {
        'x': nrm(ks[0], (BATCH, SEQ, D_MODEL), 1.0),
        'c': nrm(ks[1], (BATCH, D_MODEL), 1.0),
        'ctx': nrm(ks[2], (BATCH, CTX_LEN, D_MODEL), 1.0),
        'c_ctx': nrm(ks[3], (D_MODEL,), 1.0),
        'w_mod': nrm(ks[4], (DEPTH, D_MODEL, N_MOD * D_MODEL), 0.5 * D_MODEL ** -0.5),
        'b_mod': nrm(ks[5], (DEPTH, N_MOD * D_MODEL), 0.01),
        'norm_pre': 1.0 + nrm(ks[6], (DEPTH, 3, D_MODEL), 0.02),
        'norm_post': 1.0 + nrm(ks[7], (DEPTH, 3, D_MODEL), 0.02),
        'ffn_w_gate': nrm(ks[8], (DEPTH, 2, D_MODEL, D_FF), D_MODEL ** -0.5),
        'ffn_w_up': nrm(ks[9], (DEPTH, 2, D_MODEL, D_FF), D_MODEL ** -0.5),
        'ffn_w_down': nrm(ks[10], (DEPTH, 2, D_FF, D_MODEL), D_FF ** -0.5),
        'w_in': nrm(ks[11], (DEPTH, D_MODEL, IN_COLS), D_MODEL ** -0.5),
        'w_out': nrm(ks[12], (DEPTH, D_MIX, D_MODEL), D_MIX ** -0.5),
        'gmlp_v_gain': 1.0 + nrm(ks[13], (DEPTH, A_HEADS, A_HEAD_DIM), 0.02),
        'gmlp_w_s': nrm(ks[14], (DEPTH, A_HEADS, CHUNK, CHUNK), CHUNK ** -0.5),
        'gmlp_b_s': nrm(ks[15], (DEPTH, A_HEADS, CHUNK), 0.02),
        'diff_lambda': nrm(ks[16], (DEPTH, 4, B_QK_DIM), 0.1),
        'diff_subln': 1.0 + nrm(ks[17], (DEPTH, B_V_DIM), 0.02),
        'swa_sink': nrm(ks[18], (DEPTH, C_HEADS), 0.5),
    }


def reference(x, c, ctx, c_ctx, w_mod, b_mod, norm_pre, norm_post, ffn_w_gate, ffn_w_up, ffn_w_down,
              w_in, w_out, gmlp_v_gain, gmlp_w_s, gmlp_b_s, diff_lambda, diff_subln, swa_sink):
    b_, l_, _ = x.shape
    c_len = ctx.shape[1]
    ROWS = l_ // GRID_W
    cos_b, sin_b = axial_rope_tables(ROWS, B_QK_DIM)
    cos_c, sin_c = axial_rope_tables(ROWS, C_HEAD_DIM)
    sc = jax.nn.silu(c)
    scc = jax.nn.silu(c_ctx)
    h = ctx
    for l in range(DEPTH):
        last = l == DEPTH - 1
        mod_x = (sc @ w_mod[l] + b_mod[l]).reshape(b_, N_MOD, 1, D_MODEL)
        mod_h = (scc @ w_mod[l] + b_mod[l]).reshape(N_MOD, 1, D_MODEL)
        lam_init = 0.8 - 0.6 * math.exp(-0.3 * l)
        lam_p = diff_lambda[l].astype(jnp.float32)
        lam = (jnp.exp(jnp.sum(lam_p[0] * lam_p[1])) - jnp.exp(jnp.sum(lam_p[2] * lam_p[3])) + lam_init)
        sink = swa_sink[l].reshape(C_KV_HEADS, C_GROUP)

        f0 = (norm_pre[l, 0], norm_post[l, 0], ffn_w_gate[l, 0], ffn_w_up[l, 0], ffn_w_down[l, 0])
        x = ffn_sublayer(x, mod_x, 0, *f0)
        h = ffn_sublayer(h, mod_h, 0, *f0)

        ax = adaln_in(x, norm_pre[l, 1], mod_x[:, 3], mod_x[:, 4])
        ah = adaln_in(h, norm_pre[l, 1], mod_h[3], mod_h[4])
        uv_x, qb_x, qc_x, kb_x, vb_x, kc_x, vc_x = jnp.split(ax @ w_in[l], SPLITS, axis=-1)
        if last:
            kb_h, vb_h, kc_h, vc_h = jnp.split(ah @ w_in[l][:, KV_START:], KV_SPLITS, axis=-1)
        else:
            uv_h, qb_h, qc_h, kb_h, vb_h, kc_h, vc_h = jnp.split(ah @ w_in[l], SPLITS, axis=-1)
        kb_h = kb_h.reshape(b_, c_len, B_HEADS, 2, B_QK_DIM)
        vb_h = vb_h.reshape(b_, c_len, B_HEADS, B_V_DIM)
        kc_h = kc_h.reshape(b_, c_len, C_KV_HEADS, C_HEAD_DIM)
        vc_h = vc_h.reshape(b_, c_len, C_KV_HEADS, C_HEAD_DIM)
        qb_x = apply_rope(qb_x.reshape(b_, l_, B_HEADS, 2, B_QK_DIM), cos_b, sin_b)
        kb_x = apply_rope(kb_x.reshape(b_, l_, B_HEADS, 2, B_QK_DIM), cos_b, sin_b)
        vb_x = vb_x.reshape(b_, l_, B_HEADS, B_V_DIM)
        qc_x = apply_rope(qc_x.reshape(b_, l_, C_KV_HEADS, C_GROUP, C_HEAD_DIM), cos_c, sin_c)
        kc_x = apply_rope(kc_x.reshape(b_, l_, C_KV_HEADS, C_HEAD_DIM), cos_c, sin_c)
        vc_x = vc_x.reshape(b_, l_, C_KV_HEADS, C_HEAD_DIM)

        o_a = chunk_gmlp(uv_x, gmlp_v_gain[l], gmlp_w_s[l], gmlp_b_s[l])
        k_all = jnp.concatenate([kb_h, kb_x], axis=1)
        v_all = jnp.concatenate([vb_h, vb_x], axis=1)
        o_b = diff_post(diff_attention_latent(qb_x, k_all, v_all, lam), diff_subln[l], lam_init)
        o_c = swa_latent(qc_x, kc_x, vc_x, kc_h, vc_h, sink)
        mix_x = jnp.concatenate([o_a, o_b, o_c], axis=-1) @ w_out[l]
        x = adaln_out(x, mix_x, norm_post[l, 1], mod_x[:, 5], 1.0)
        if not last:
            qb_h = qb_h.reshape(b_, c_len, B_HEADS, 2, B_QK_DIM)
            qc_h = qc_h.reshape(b_, c_len, C_KV_HEADS, C_GROUP, C_HEAD_DIM)
            oh_a = chunk_gmlp(uv_h, gmlp_v_gain[l], gmlp_w_s[l], gmlp_b_s[l])
            oh_b = diff_post(diff_softmax(qb_h, kb_h, vb_h, lam), diff_subln[l], lam_init)
            oh_c = swa_context(qc_h, kc_h, vc_h, sink)
            mix_h = jnp.concatenate([oh_a, oh_b, oh_c], axis=-1) @ w_out[l]
            h = adaln_out(h, mix_h, norm_post[l, 1], mod_h[5], 1.0)

        f1 = (norm_pre[l, 2], norm_post[l, 2], ffn_w_gate[l, 1], ffn_w_up[l, 1], ffn_w_down[l, 1])
        x = ffn_sublayer(x, mod_x, 2, *f1)
        if not last:
            h = ffn_sublayer(h, mod_h, 2, *f1)
    return x
```

```python
import functools
import math

import jax
import jax.numpy as jnp
from jax import lax
from jax.experimental import pallas as pl
from jax.experimental.pallas import tpu as pltpu

D_MODEL = 1024
BATCH = 8
SEQ = 2048
DEPTH = 4
GRID_W = 64
CTX_LEN = 256
N_MOD = 9
D_FF = int(math.ceil(8 * D_MODEL / 3 / 128)) * 128
FFN_RES = 0.5
ROPE_BASE = 10000.0
EPS = 1e-6
NEG_INF = -1e30

A_WIDTH = D_MODEL // 4
A_HEADS = 4
A_HEAD_DIM = A_WIDTH // A_HEADS
CHUNK = 128
B_WIDTH = D_MODEL // 2
B_HEADS = 4
B_V_DIM = B_WIDTH // B_HEADS
B_QK_DIM = B_V_DIM // 2
C_WIDTH = D_MODEL // 4
C_HEAD_DIM = 64
C_HEADS = C_WIDTH // C_HEAD_DIM
C_KV_HEADS = 2
C_GROUP = C_HEADS // C_KV_HEADS
WINDOW = 128
D_MIX = A_WIDTH + B_WIDTH + C_WIDTH

OFF_UV = 0
OFF_QB = OFF_UV + 2 * A_WIDTH
OFF_QC = OFF_QB + B_HEADS * 2 * B_QK_DIM
OFF_KB = OFF_QC + C_HEADS * C_HEAD_DIM
OFF_VB = OFF_KB + B_HEADS * 2 * B_QK_DIM
OFF_KC = OFF_VB + B_HEADS * B_V_DIM
OFF_VC = OFF_KC + C_KV_HEADS * C_HEAD_DIM
IN_COLS = OFF_VC + C_KV_HEADS * C_HEAD_DIM

T_ALL = CTX_LEN + SEQ
LANES = 128
TM = 256
N_TILES = T_ALL // TM
MOD_ROWS = 16
CTX_ROW = BATCH
MOD_TN = 1152
BAND = 2 * WINDOW + TM
VMEM_LIMIT = 56 * 1024 * 1024

F32 = jnp.float32
BF16 = jnp.bfloat16


def _cparams(n_axes):
    return pltpu.CompilerParams(dimension_semantics=("arbitrary",) * n_axes, vmem_limit_bytes=VMEM_LIMIT)


def _rms(xf, g):
    ms = jnp.mean(xf * xf, axis=-1, keepdims=True)
    return xf * lax.rsqrt(ms + EPS) * g


def _dot(a, b):
    return jnp.dot(a, b, preferred_element_type=F32)


def _dot_nt(a, b):
    return lax.dot_general(a, b, (((1,), (1,)), ((), ())), preferred_element_type=F32)


def _mod_kernel(c_ref, w_ref, b_ref, o_ref):
    c = c_ref[...]
    sc = c * jax.nn.sigmoid(c)
    o_ref[...] = _dot(sc.astype(BF16), w_ref[...].astype(BF16)) + b_ref[...]


def _modulation(c_all, w_mod, b_mod):
    n_cols = N_MOD * D_MODEL
    return pl.pallas_call(
        _mod_kernel,
        grid=(DEPTH, n_cols // MOD_TN),
        in_specs=[
            pl.BlockSpec((MOD_ROWS, D_MODEL), lambda l, n: (0, 0)),
            pl.BlockSpec((None, D_MODEL, MOD_TN), lambda l, n: (l, 0, n)),
            pl.BlockSpec((None, 1, MOD_TN), lambda l, n: (l, 0, n)),
        ],
        out_specs=pl.BlockSpec((None, MOD_ROWS, MOD_TN), lambda l, n: (l, 0, n)),
        out_shape=jax.ShapeDtypeStruct((DEPTH, MOD_ROWS, n_cols), F32),
        compiler_params=_cparams(2),
        name="modulation",
    )(c_all, w_mod, b_mod.reshape(DEPTH, 1, n_cols))


def _mod_spec(layer, t_off):
    def index(b, t):
        return (layer, jnp.where(t + t_off == 0, CTX_ROW, b), 0, 0)
    return pl.BlockSpec((None, None, N_MOD, D_MODEL), index)


def _const_spec(shape):
    return pl.BlockSpec(shape, lambda b, t: (0,) * len(shape))


def _ffn_kernel(s_ref, mod_ref, gpre_ref, gpost_ref, wg_ref, wu_ref, wd_ref, o_ref, *, j):
    xf = s_ref[...]
    shift = mod_ref[3 * j:3 * j + 1, :]
    scale = mod_ref[3 * j + 1:3 * j + 2, :]
    gate = mod_ref[3 * j + 2:3 * j + 3, :]
    y = (_rms(xf, gpre_ref[...]) * (1.0 + scale) + shift).astype(BF16)
    g = _dot(y, wg_ref[...])
    u = _dot(y, wu_ref[...])
    h = (g * jax.nn.sigmoid(g) * u).astype(BF16)
    o = _dot(h, wd_ref[...])
    o_ref[...] = xf + (FFN_RES * gate) * _rms(o, gpost_ref[...])


def _ffn(s, mod, layer, j, g_pre, g_post, wg, wu, wd):
    rows = s.shape[1]
    t_off = (T_ALL - rows) // TM
    tile = pl.BlockSpec((None, TM, D_MODEL), lambda b, t: (b, t, 0))
    return pl.pallas_call(
        functools.partial(_ffn_kernel, j=j),
        grid=(BATCH, rows // TM),
        in_specs=[
            tile,
            _mod_spec(layer, t_off),
            _const_spec((1, D_MODEL)),
            _const_spec((1, D_MODEL)),
            _const_spec((D_MODEL, D_FF)),
            _const_spec((D_MODEL, D_FF)),
            _const_spec((D_FF, D_MODEL)),
        ],
        out_specs=tile,
        out_shape=jax.ShapeDtypeStruct((BATCH, rows, D_MODEL), F32),
        compiler_params=_cparams(2),
        name="ffn",
    )(s, mod, g_pre.reshape(1, D_MODEL), g_post.reshape(1, D_MODEL), wg, wu, wd)


def _mix_in_kernel(s_ref, mod_ref, gpre_ref, win_ref, cos_ref, sina_ref, sinb_ref, vgain_ref, ws_ref, bs_ref,
                   seg_ref, oa_ref, qb_ref, qc_ref, kb_ref, vb_ref, kc_ref, vc_ref):
    xf = s_ref[...]
    shift = mod_ref[3:4, :]
    scale = mod_ref[4:5, :]
    ax = (_rms(xf, gpre_ref[...]) * (1.0 + scale) + shift).astype(BF16)
    proj = _dot(ax, win_ref[...])

    cos = cos_ref[...]
    sin_a = sina_ref[...]
    sin_b = sinb_ref[...]
    quarter = C_HEAD_DIM // 4

    def rope(xb):
        return (xb * cos + pltpu.roll(xb, LANES - quarter, 1) * sin_a + pltpu.roll(xb, quarter, 1) * sin_b)

    def block(off, i):
        return proj[:, off + i * LANES:off + (i + 1) * LANES]

    lane = lax.broadcasted_iota(jnp.int32, (TM, LANES), 1)
    low = lane < (LANES // 2)

    q_scale = B_QK_DIM ** -0.5
    for h in range(B_HEADS):
        q = rope(block(OFF_QB, h)) * q_scale
        qb_ref[:, (2 * h) * LANES:(2 * h + 1) * LANES] = jnp.where(low, q, 0.0).astype(BF16)
        qb_ref[:, (2 * h + 1) * LANES:(2 * h + 2) * LANES] = jnp.where(low, 0.0, q).astype(BF16)
        kb_ref[:, h * LANES:(h + 1) * LANES] = rope(block(OFF_KB, h)).astype(BF16)
    vb_ref[...] = proj[:, OFF_VB:OFF_VB + B_WIDTH].astype(BF16)

    c_scale = C_HEAD_DIM ** -0.5
    for kv in range(C_KV_HEADS):
        q = rope(block(OFF_QC, kv)) * c_scale
        q_sw = pltpu.roll(q, LANES // 2, 1)
        own_low = kv == 0
        g0 = jnp.where(low, q, 0.0) if own_low else jnp.where(low, 0.0, q_sw)
        g1 = jnp.where(low, q_sw, 0.0) if own_low else jnp.where(low, 0.0, q)
        qc_ref[:, (2 * kv) * LANES:(2 * kv + 1) * LANES] = g0.astype(BF16)
        qc_ref[:, (2 * kv + 1) * LANES:(2 * kv + 2) * LANES] = g1.astype(BF16)
    kc_ref[...] = rope(block(OFF_KC, 0)).astype(BF16)
    vc_ref[...] = proj[:, OFF_VC:OFF_VC + LANES].astype(BF16)

    uv = jax.nn.gelu(proj[:, OFF_UV:OFF_UV + 2 * A_WIDTH], approximate=True)
    u = uv[:, :A_WIDTH]
    v = uv[:, A_WIDTH:]
    v2 = v * v
    v2_hi = v2.astype(BF16)
    v2_lo = (v2 - v2_hi.astype(F32)).astype(BF16)
    seg = seg_ref[...]
    ms = (_dot(v2_hi, seg) + _dot(v2_lo, seg)) * (1.0 / A_HEAD_DIM)
    vn = (v * lax.rsqrt(ms + EPS) * vgain_ref[...]).astype(BF16)
    a_lane = lax.broadcasted_iota(jnp.int32, (CHUNK, A_WIDTH), 1)
    for c in range(TM // CHUNK):
        vc = vn[c * CHUNK:(c + 1) * CHUNK, :]
        mixed = _dot(ws_ref[A_HEADS - 1], vc)
        for h in range(A_HEADS - 2, -1, -1):
            mixed = jnp.where(a_lane < (h + 1) * A_HEAD_DIM, _dot(ws_ref[h], vc), mixed)
        mixed = mixed + bs_ref[...]
        oa_ref[c * CHUNK:(c + 1) * CHUNK, :] = (u[c * CHUNK:(c + 1) * CHUNK, :] * mixed).astype(BF16)


def _mix_in(s, mod, layer, g_pre, w_in, tables, v_gain, w_s, b_s, seg):
    cos_t, sin_a, sin_b = tables
    tok = lambda width: pl.BlockSpec((None, TM, width), lambda b, t: (b, t, 0))
    pos = pl.BlockSpec((TM, LANES), lambda b, t: (t, 0))
    out_widths = (A_WIDTH, 2 * B_WIDTH, 2 * C_WIDTH, B_WIDTH, B_WIDTH, LANES, LANES)
    return pl.pallas_call(
        _mix_in_kernel,
        grid=(BATCH, N_TILES),
        in_specs=[
            tok(D_MODEL),
            _mod_spec(layer, 0),
            _const_spec((1, D_MODEL)),
            _const_spec((D_MODEL, IN_COLS)),
            pos, pos, pos,
            _const_spec((1, A_WIDTH)),
            _const_spec((A_HEADS, CHUNK, CHUNK)),
            _const_spec((CHUNK, A_WIDTH)),
            _const_spec((A_WIDTH, A_WIDTH)),
        ],
        out_specs=[tok(w) for w in out_widths],
        out_shape=[jax.ShapeDtypeStruct((BATCH, T_ALL, w), BF16) for w in out_widths],
        compiler_params=_cparams(2),
        name="mix_in",
    )(s, mod, g_pre.reshape(1, D_MODEL), w_in, cos_t, sin_a, sin_b, v_gain.reshape(1, A_WIDTH), w_s, b_s, seg)


def _mix_attn_kernel(s_ref, oa_ref, qb_ref, qc_ref, kb_ref, vb_ref, kc_ref, vc_ref, mod_ref, gpost_ref,
                     subln_ref, lamp_ref, sink_ref, wout_ref, o_ref, *, t_off, lam_init):
    t = pl.program_id(1) + t_off

    lam_p = lamp_ref[...]
    lam = (jnp.exp(jnp.sum(lam_p[0:1, :] * lam_p[1:2, :], axis=-1, keepdims=True))
           - jnp.exp(jnp.sum(lam_p[2:3, :] * lam_p[3:4, :], axis=-1, keepdims=True)) + lam_init)

    row2 = lax.broadcasted_iota(jnp.int32, (2 * TM, 1), 0)
    lane = lax.broadcasted_iota(jnp.int32, (TM, LANES), 1)
    low = lane < (LANES // 2)

    def diff_heads(n_keys):
        outs = []
        for h in range(B_HEADS):
            qq = jnp.concatenate([qb_ref[:, (2 * h) * LANES:(2 * h + 1) * LANES],
                                  qb_ref[:, (2 * h + 1) * LANES:(2 * h + 2) * LANES]], axis=0)
            s = _dot_nt(qq, kb_ref[0:n_keys, h * LANES:(h + 1) * LANES])
            p = jnp.exp(s - jnp.max(s, axis=-1, keepdims=True))
            r = 1.0 / jnp.sum(p, axis=-1, keepdims=True)
            w = p[:TM] * r[:TM] - p[TM:] * (lam * r[TM:])
            o = _dot(w.astype(BF16), vb_ref[0:n_keys, h * LANES:(h + 1) * LANES])
            outs.append((_rms(o, subln_ref[...]) * (1.0 - lam_init)).astype(BF16))
        return outs

    def swa_heads(band_start, q_pos0):
        outs = []
        k_ctx = kc_ref[0:CTX_LEN, :]
        v_ctx = vc_ref[0:CTX_LEN, :]
        if band_start is not None:
            k_band = kc_ref[pl.ds(band_start, BAND), :]
            v_band = vc_ref[pl.ds(band_start, BAND), :]
            qi = lax.broadcasted_iota(jnp.int32, (2 * TM, BAND), 0) & (TM - 1)
            kj = lax.broadcasted_iota(jnp.int32, (2 * TM, BAND), 1)
            dist = qi - kj + (q_pos0 - (band_start - CTX_LEN))
            valid = jnp.abs(dist) <= WINDOW
        for kv in range(C_KV_HEADS):
            qq = jnp.concatenate([qc_ref[:, (2 * kv) * LANES:(2 * kv + 1) * LANES],
                                  qc_ref[:, (2 * kv + 1) * LANES:(2 * kv + 2) * LANES]], axis=0)
            sink = jnp.where(row2 < TM, sink_ref[0:1, 2 * kv:2 * kv + 1], sink_ref[0:1, 2 * kv + 1:2 * kv + 2])
            s_ctx = _dot_nt(qq, k_ctx)
            m = jnp.maximum(jnp.max(s_ctx, axis=-1, keepdims=True), sink)
            if band_start is not None:
                s_band = jnp.where(valid, _dot_nt(qq, k_band), NEG_INF)
                m = jnp.maximum(m, jnp.max(s_band, axis=-1, keepdims=True))
                p_band = jnp.exp(s_band - m)
            p_ctx = jnp.exp(s_ctx - m)
            denom = jnp.sum(p_ctx, axis=-1, keepdims=True) + jnp.exp(sink - m)
            if band_start is not None:
                denom = denom + jnp.sum(p_band, axis=-1, keepdims=True)
            r = 1.0 / denom
            o = _dot((p_ctx * r).astype(BF16), v_ctx)
            if band_start is not None:
                o = o + _dot((p_band * r).astype(BF16), v_band)
            o_g0, o_g1 = o[:TM], o[TM:]
            if kv == 0:
                blk = jnp.where(low, o_g0, pltpu.roll(o_g1, LANES // 2, 1))
            else:
                blk = jnp.where(low, pltpu.roll(o_g0, LANES // 2, 1), o_g1)
            outs.append(blk.astype(BF16))
        return outs

    def finish(ob, oc):
        cat = jnp.concatenate([oa_ref[...]] + ob + oc, axis=-1)
        mix = _dot(cat, wout_ref[...])
        gate = mod_ref[5:6, :]
        o_ref[...] = s_ref[...] + gate * _rms(mix, gpost_ref[...])

    def latent_tile():
        q_pos0 = (t - 1) * TM
        band_l = jnp.clip(q_pos0 - WINDOW, 0, SEQ - BAND)
        band_start = pl.multiple_of(band_l + CTX_LEN, LANES)
        finish(diff_heads(T_ALL), swa_heads(band_start, q_pos0))

    def context_tile():
        finish(diff_heads(CTX_LEN), swa_heads(None, None))

    if t_off == 0:
        pl.when(t == 0)(context_tile)
        pl.when(t > 0)(latent_tile)
    else:
        latent_tile()


def _mix_attn(s, oa, qb, qc, kb, vb, kc, vc, mod, layer, g_post, subln, lam_p, sink, w_out, *, t_off, lam_init):
    n_t = N_TILES - t_off
    tok = lambda width: pl.BlockSpec((None, TM, width), lambda b, t: (b, t + t_off, 0))
    slab = lambda width: pl.BlockSpec((None, T_ALL, width), lambda b, t: (b, 0, 0))
    return pl.pallas_call(
        functools.partial(_mix_attn_kernel, t_off=t_off, lam_init=lam_init),
        grid=(BATCH, n_t),
        in_specs=[
            tok(D_MODEL), tok(A_WIDTH), tok(2 * B_WIDTH), tok(2 * C_WIDTH),
            slab(B_WIDTH), slab(B_WIDTH), slab(LANES), slab(LANES),
            _mod_spec(layer, t_off),
            _const_spec((1, D_MODEL)),
            _const_spec((1, B_V_DIM)),
            _const_spec((4, B_QK_DIM)),
            _const_spec((1, C_HEADS)),
            _const_spec((D_MIX, D_MODEL)),
        ],
        out_specs=pl.BlockSpec((None, TM, D_MODEL), lambda b, t: (b, t, 0)),
        out_shape=jax.ShapeDtypeStruct((BATCH, n_t * TM, D_MODEL), F32),
        compiler_params=_cparams(2),
        name="mix_attn",
    )(s, oa, qb, qc, kb, vb, kc, vc, mod, g_post.reshape(1, D_MODEL), subln.reshape(1, B_V_DIM), lam_p,
      sink.reshape(1, C_HEADS), w_out)


def _rope_tables():
    rows = SEQ // GRID_W
    row = jnp.repeat(jnp.arange(rows, dtype=F32), GRID_W)
    col = jnp.tile(jnp.arange(GRID_W, dtype=F32), rows)
    quarter = C_HEAD_DIM // 4
    inv = ROPE_BASE ** (-jnp.arange(quarter, dtype=F32) / quarter)
    ar = row[:, None] * inv[None, :]
    ac = col[:, None] * inv[None, :]
    ang = jnp.concatenate([ar, ar, ac, ac] * (LANES // C_HEAD_DIM), axis=-1)
    cos = jnp.concatenate([jnp.ones((CTX_LEN, LANES), F32), jnp.cos(ang)], axis=0)
    sin = jnp.concatenate([jnp.zeros((CTX_LEN, LANES), F32), jnp.sin(ang)], axis=0)
    first = (jnp.arange(LANES) // quarter) % 2 == 0
    sin_a = jnp.where(first[None, :], -sin, 0.0)
    sin_b = jnp.where(first[None, :], 0.0, sin)
    return cos, sin_a, sin_b


def kernel(x, c, ctx, c_ctx, w_mod, b_mod, norm_pre, norm_post, ffn_w_gate, ffn_w_up, ffn_w_down, w_in, w_out,
           gmlp_v_gain, gmlp_w_s, gmlp_b_s, diff_lambda, diff_subln, swa_sink):
    assert x.shape == (BATCH, SEQ, D_MODEL) and ctx.shape == (BATCH, CTX_LEN, D_MODEL)
    s = jnp.concatenate([ctx, x], axis=1)
    c_all = jnp.zeros((MOD_ROWS, D_MODEL), F32).at[:BATCH].set(c).at[CTX_ROW].set(c_ctx)
    mod = _modulation(c_all, w_mod, b_mod).reshape(DEPTH, MOD_ROWS, N_MOD, D_MODEL)

    tables = _rope_tables()
    head_of = jnp.arange(A_WIDTH) // A_HEAD_DIM
    seg = (head_of[:, None] == head_of[None, :]).astype(BF16)
    wg, wu, wd = ffn_w_gate.astype(BF16), ffn_w_up.astype(BF16), ffn_w_down.astype(BF16)
    w_in_b, w_out_b, w_s_b = w_in.astype(BF16), w_out.astype(BF16), gmlp_w_s.astype(BF16)
    b_s = jnp.repeat(jnp.swapaxes(gmlp_b_s, 1, 2), A_HEAD_DIM, axis=2)

    for l in range(DEPTH):
        last = l == DEPTH - 1
        lam_init = 0.8 - 0.6 * math.exp(-0.3 * l)
        s = _ffn(s, mod, l, 0, norm_pre[l, 0], norm_post[l, 0], wg[l, 0], wu[l, 0], wd[l, 0])
        oa, qb, qc, kb, vb, kc, vc = _mix_in(s, mod, l, norm_pre[l, 1], w_in_b[l], tables, gmlp_v_gain[l],
                                             w_s_b[l], b_s[l], seg)
        s = _mix_attn(s, oa, qb, qc, kb, vb, kc, vc, mod, l, norm_post[l, 1], diff_subln[l], diff_lambda[l],
                      swa_sink[l], w_out_b[l], t_off=1 if last else 0, lam_init=lam_init)
        s = _ffn(s, mod, l, 2, norm_pre[l, 2], norm_post[l, 2], wg[l, 1], wu[l, 1], wd[l, 1])
    return s
```

```python
import functools
import math

import jax
import jax.numpy as jnp
from jax import lax
from jax.experimental import pallas as pl
from jax.experimental.pallas import tpu as pltpu

D_MODEL = 1024
BATCH = 8
SEQ = 2048
DEPTH = 4
GRID_W = 64
CTX_LEN = 256
N_MOD = 9
D_FF = int(math.ceil(8 * D_MODEL / 3 / 128)) * 128
FFN_RES = 0.5
ROPE_BASE = 10000.0
EPS = 1e-6
NEG_INF = -1e30

A_WIDTH = D_MODEL // 4
A_HEADS = 4
A_HEAD_DIM = A_WIDTH // A_HEADS
CHUNK = 128
B_WIDTH = D_MODEL // 2
B_HEADS = 4
B_V_DIM = B_WIDTH // B_HEADS
B_QK_DIM = B_V_DIM // 2
C_WIDTH = D_MODEL // 4
C_HEAD_DIM = 64
C_HEADS = C_WIDTH // C_HEAD_DIM
C_KV_HEADS = 2
C_GROUP = C_HEADS // C_KV_HEADS
WINDOW = 128
D_MIX = A_WIDTH + B_WIDTH + C_WIDTH

OFF_UV = 0
OFF_QB = OFF_UV + 2 * A_WIDTH
OFF_QC = OFF_QB + B_HEADS * 2 * B_QK_DIM
OFF_KB = OFF_QC + C_HEADS * C_HEAD_DIM
OFF_VB = OFF_KB + B_HEADS * 2 * B_QK_DIM
OFF_KC = OFF_VB + B_HEADS * B_V_DIM
OFF_VC = OFF_KC + C_KV_HEADS * C_HEAD_DIM
IN_COLS = OFF_VC + C_KV_HEADS * C_HEAD_DIM

T_ALL = CTX_LEN + SEQ
LANES = 128
TM = 256
N_TILES = T_ALL // TM
MOD_ROWS = 16
CTX_ROW = BATCH
MOD_TN = 1152
FFN_SUB_CTX = 3
FFN_SUB_LATENT = 2
BAND = 2 * WINDOW + TM
ROW_BLK = 16
VMEM_LIMIT = 56 * 1024 * 1024

F32 = jnp.float32
BF16 = jnp.bfloat16


def _cparams(n_axes):
    return pltpu.CompilerParams(dimension_semantics=("arbitrary",) * n_axes, vmem_limit_bytes=VMEM_LIMIT)


def _rms(xf, g):
    ms = jnp.mean(xf * xf, axis=-1, keepdims=True)
    return xf * lax.rsqrt(ms + EPS) * g


def _dot(a, b):
    return jnp.dot(a, b, preferred_element_type=F32)


def _dot_nt(a, b):
    return lax.dot_general(a, b, (((1,), (1,)), ((), ())), preferred_element_type=F32)


def _mod_kernel(c_ref, w_ref, b_ref, o_ref):
    c = c_ref[...]
    sc = c * jax.nn.sigmoid(c)
    o_ref[...] = _dot(sc.astype(BF16), w_ref[...].astype(BF16)) + b_ref[...]


def _modulation(c_all, w_mod, b_mod):
    n_cols = N_MOD * D_MODEL
    return pl.pallas_call(
        _mod_kernel,
        grid=(DEPTH, n_cols // MOD_TN),
        in_specs=[
            pl.BlockSpec((MOD_ROWS, D_MODEL), lambda l, n: (0, 0)),
            pl.BlockSpec((None, D_MODEL, MOD_TN), lambda l, n: (l, 0, n)),
            pl.BlockSpec((None, 1, MOD_TN), lambda l, n: (l, 0, n)),
        ],
        out_specs=pl.BlockSpec((None, MOD_ROWS, MOD_TN), lambda l, n: (l, 0, n)),
        out_shape=jax.ShapeDtypeStruct((DEPTH, MOD_ROWS, n_cols), F32),
        compiler_params=_cparams(2),
        name="modulation",
    )(c_all, w_mod, b_mod.reshape(DEPTH, 1, n_cols))


def _mod_spec(layer, t_off):
    def index(b, t):
        return (layer, jnp.where(t + t_off == 0, CTX_ROW, b), 0, 0)
    return pl.BlockSpec((None, None, N_MOD, D_MODEL), index)


def _const_spec(shape):
    return pl.BlockSpec(shape, lambda b, t: (0,) * len(shape))


def _ffn_kernel(*refs, j, n_sub, has_ctx, split_in):
    if split_in:
        ctx_ref, x_refs, rest = refs[0], refs[1:1 + n_sub], refs[1 + n_sub:]
    else:
        s_ref, rest = refs[0], refs[1:]
    modb_ref, modc_ref, gpre_ref, gpost_ref, wg_ref, wu_ref, wd_ref, o_ref = rest
    first = pl.program_id(1) == 0
    for i in range(n_sub):
        rows = slice(i * TM, (i + 1) * TM)
        ctx_rows = has_ctx and i == 0
        if split_in:
            xf = x_refs[i][...]
            if ctx_rows:
                xf = jnp.where(first, ctx_ref[...], xf)
        else:
            xf = s_ref[rows, :]

        def mod_row(k):
            row = modb_ref[k:k + 1, :]
            return jnp.where(first, modc_ref[k:k + 1, :], row) if ctx_rows else row

        shift, scale, gate = mod_row(3 * j), mod_row(3 * j + 1), mod_row(3 * j + 2)
        y = (_rms(xf, gpre_ref[...]) * (1.0 + scale) + shift).astype(BF16)
        g = _dot(y, wg_ref[...])
        u = _dot(y, wu_ref[...])
        h = (g * jax.nn.sigmoid(g) * u).astype(BF16)
        o = _dot(h, wd_ref[...])
        o_ref[rows, :] = xf + (FFN_RES * gate) * _rms(o, gpost_ref[...])


def _mod_specs(layer):
    batch_row = pl.BlockSpec((None, None, N_MOD, D_MODEL), lambda b, t: (layer, b, 0, 0))
    ctx_row = pl.BlockSpec((None, None, N_MOD, D_MODEL), lambda b, t: (layer, CTX_ROW, 0, 0))
    return [batch_row, ctx_row]


def _resident(block, index):
    return pl.BlockSpec(block, lambda b, t: index, pipeline_mode=pl.Buffered(1))


def _ffn(s, mod, layer, j, jj, norm_pre, norm_post, wg, wu, wd, ctx=None):
    split_in = ctx is not None
    rows = T_ALL if split_in else s.shape[1]
    has_ctx = rows == T_ALL
    n_sub = FFN_SUB_CTX if has_ctx else FFN_SUB_LATENT
    tmf = n_sub * TM
    tile = pl.BlockSpec((None, tmf, D_MODEL), lambda b, t: (b, t, 0))
    if split_in:
        def latent_tile(i):
            return pl.BlockSpec((None, TM, D_MODEL), lambda b, t: (b, jnp.maximum(n_sub * t + i - 1, 0), 0))
        data_specs = [pl.BlockSpec((None, TM, D_MODEL), lambda b, t: (b, 0, 0))]
        data_specs += [latent_tile(i) for i in range(n_sub)]
        data = [ctx] + [s] * n_sub
    else:
        data_specs, data = [tile], [s]
    return pl.pallas_call(
        functools.partial(_ffn_kernel, j=j, n_sub=n_sub, has_ctx=has_ctx, split_in=split_in),
        grid=(BATCH, rows // tmf),
        in_specs=data_specs + _mod_specs(layer) + [
            _resident((None, None, 1, D_MODEL), (layer, j, 0, 0)),
            _resident((None, None, 1, D_MODEL), (layer, j, 0, 0)),
            _resident((None, None, D_MODEL, D_FF), (layer, jj, 0, 0)),
            _resident((None, None, D_MODEL, D_FF), (layer, jj, 0, 0)),
            _resident((None, None, D_FF, D_MODEL), (layer, jj, 0, 0)),
        ],
        out_specs=tile,
        out_shape=jax.ShapeDtypeStruct((BATCH, rows, D_MODEL), F32),
        compiler_params=_cparams(2),
        name="ffn",
    )(*data, mod, mod, norm_pre, norm_post, wg, wu, wd)


def _mix_in_kernel(s_ref, mod_ref, gpre_ref, win_ref, cos_ref, sina_ref, sinb_ref, vgain_ref, ws_ref, bs_ref,
                   seg_ref, oa_ref, qb_ref, qc_ref, kb_ref, vb_ref, kc_ref, vc_ref):
    xf = s_ref[...]
    shift = mod_ref[3:4, :]
    scale = mod_ref[4:5, :]
    ax = (_rms(xf, gpre_ref[...]) * (1.0 + scale) + shift).astype(BF16)
    proj = _dot(ax, win_ref[...])

    cos = cos_ref[...]
    sin_a = sina_ref[...]
    sin_b = sinb_ref[...]
    quarter = C_HEAD_DIM // 4

    def rope(xb):
        return (xb * cos + pltpu.roll(xb, LANES - quarter, 1) * sin_a + pltpu.roll(xb, quarter, 1) * sin_b)

    def block(off, i):
        return proj[:, off + i * LANES:off + (i + 1) * LANES]

    lane = lax.broadcasted_iota(jnp.int32, (TM, LANES), 1)
    low = lane < (LANES // 2)

    q_scale = B_QK_DIM ** -0.5
    for h in range(B_HEADS):
        q = rope(block(OFF_QB, h)) * q_scale
        qb_ref[:, (2 * h) * LANES:(2 * h + 1) * LANES] = jnp.where(low, q, 0.0).astype(BF16)
        qb_ref[:, (2 * h + 1) * LANES:(2 * h + 2) * LANES] = jnp.where(low, 0.0, q).astype(BF16)
        kb_ref[:, h * LANES:(h + 1) * LANES] = rope(block(OFF_KB, h)).astype(BF16)
    vb_ref[...] = proj[:, OFF_VB:OFF_VB + B_WIDTH].astype(BF16)

    c_scale = C_HEAD_DIM ** -0.5
    for kv in range(C_KV_HEADS):
        q = rope(block(OFF_QC, kv)) * c_scale
        q_sw = pltpu.roll(q, LANES // 2, 1)
        own_low = kv == 0
        g0 = jnp.where(low, q, 0.0) if own_low else jnp.where(low, 0.0, q_sw)
        g1 = jnp.where(low, q_sw, 0.0) if own_low else jnp.where(low, 0.0, q)
        qc_ref[:, (2 * kv) * LANES:(2 * kv + 1) * LANES] = g0.astype(BF16)
        qc_ref[:, (2 * kv + 1) * LANES:(2 * kv + 2) * LANES] = g1.astype(BF16)
    kc_ref[...] = rope(block(OFF_KC, 0)).astype(BF16)
    vc_ref[...] = proj[:, OFF_VC:OFF_VC + LANES].astype(BF16)

    uv = jax.nn.gelu(proj[:, OFF_UV:OFF_UV + 2 * A_WIDTH], approximate=True)
    u = uv[:, :A_WIDTH]
    v = uv[:, A_WIDTH:]
    v2 = v * v
    v2_hi = v2.astype(BF16)
    v2_lo = (v2 - v2_hi.astype(F32)).astype(BF16)
    seg = seg_ref[...]
    ms = (_dot(v2_hi, seg) + _dot(v2_lo, seg)) * (1.0 / A_HEAD_DIM)
    vn = (v * lax.rsqrt(ms + EPS) * vgain_ref[...]).astype(BF16)
    a_lane = lax.broadcasted_iota(jnp.int32, (CHUNK, A_WIDTH), 1)
    for c in range(TM // CHUNK):
        vc = vn[c * CHUNK:(c + 1) * CHUNK, :]
        mixed = _dot(ws_ref[A_HEADS - 1], vc)
        for h in range(A_HEADS - 2, -1, -1):
            mixed = jnp.where(a_lane < (h + 1) * A_HEAD_DIM, _dot(ws_ref[h], vc), mixed)
        mixed = mixed + bs_ref[...]
        oa_ref[c * CHUNK:(c + 1) * CHUNK, :] = (u[c * CHUNK:(c + 1) * CHUNK, :] * mixed).astype(BF16)


def _mix_in(s, mod, layer, g_pre, w_in, tables, v_gain, w_s, b_s, seg):
    cos_t, sin_a, sin_b = tables
    tok = lambda width: pl.BlockSpec((None, TM, width), lambda b, t: (b, t, 0))
    pos = pl.BlockSpec((TM, LANES), lambda b, t: (t, 0))
    out_widths = (A_WIDTH, 2 * B_WIDTH, 2 * C_WIDTH, B_WIDTH, B_WIDTH, LANES, LANES)
    return pl.pallas_call(
        _mix_in_kernel,
        grid=(BATCH, N_TILES),
        in_specs=[
            tok(D_MODEL),
            _mod_spec(layer, 0),
            _const_spec((1, D_MODEL)),
            _const_spec((D_MODEL, IN_COLS)),
            pos, pos, pos,
            _const_spec((1, A_WIDTH)),
            _const_spec((A_HEADS, CHUNK, CHUNK)),
            _const_spec((CHUNK, A_WIDTH)),
            _const_spec((A_WIDTH, A_WIDTH)),
        ],
        out_specs=[tok(w) for w in out_widths],
        out_shape=[jax.ShapeDtypeStruct((BATCH, T_ALL, w), BF16) for w in out_widths],
        compiler_params=_cparams(2),
        name="mix_in",
    )(s, mod, g_pre.reshape(1, D_MODEL), w_in, cos_t, sin_a, sin_b, v_gain.reshape(1, A_WIDTH), w_s, b_s, seg)


def _mix_attn_kernel(s_ref, oa_ref, qb_ref, qc_ref, kb_ref, vb_ref, kc_ref, vc_ref, mod_ref, gpost_ref,
                     subln_ref, lamp_ref, sink_ref, wout_ref, o_ref, *scratch, t_off, lam_init):
    t = pl.program_id(1) + t_off
    s_scrs, p_scrs = scratch[:2], scratch[2:]

    lam_p = lamp_ref[...]
    lam = (jnp.exp(jnp.sum(lam_p[0:1, :] * lam_p[1:2, :], axis=-1, keepdims=True))
           - jnp.exp(jnp.sum(lam_p[2:3, :] * lam_p[3:4, :], axis=-1, keepdims=True)) + lam_init)

    row2 = lax.broadcasted_iota(jnp.int32, (2 * TM, 1), 0)
    lane = lax.broadcasted_iota(jnp.int32, (TM, LANES), 1)
    low = lane < (LANES // 2)

    def diff_heads(n_keys, mix):
        pair = 2 * LANES
        outs = []

        def scores(h):
            qq = jnp.concatenate([qb_ref[:, (2 * h) * LANES:(2 * h + 1) * LANES],
                                  qb_ref[:, (2 * h + 1) * LANES:(2 * h + 2) * LANES]], axis=0)
            s_scrs[h % 2][:, 0:n_keys] = _dot_nt(qq, kb_ref[0:n_keys, h * LANES:(h + 1) * LANES])

        scores(0)
        for h in range(B_HEADS):
            s_scr, p_scr = s_scrs[h % 2], p_scrs[h % 2]
            head_cols = slice(h * LANES, (h + 1) * LANES)
            if h + 1 < B_HEADS:
                scores(h + 1)
            m = jnp.max(s_scr[:, 0:n_keys], axis=-1, keepdims=True)
            l_parts = []
            for rb in range(2 * TM // ROW_BLK):
                rows = slice(rb * ROW_BLK, (rb + 1) * ROW_BLK)
                m_blk = m[rows]
                acc = None
                for c in range(n_keys // LANES):
                    cols = slice(c * LANES, (c + 1) * LANES)
                    e = jnp.exp(s_scr[rows, cols] - m_blk)
                    acc = e if acc is None else acc + e
                    p_scr[rows, cols] = e.astype(BF16)
                l_parts.append(acc)
            r = 1.0 / jnp.sum(jnp.concatenate(l_parts, axis=0), axis=-1, keepdims=True)
            pv = _dot(p_scr[:, 0:n_keys], vb_ref[0:n_keys, head_cols])
            o = pv[:TM] * r[:TM] - pv[TM:] * (lam * r[TM:])
            outs.append((_rms(o, subln_ref[...]) * (1.0 - lam_init)).astype(BF16))
            if h % 2 == 1:
                rows = slice(A_WIDTH + (h // 2) * pair, A_WIDTH + (h // 2 + 1) * pair)
                mix = mix + _dot(jnp.concatenate(outs[h - 1:h + 1], axis=-1), wout_ref[rows, :])
        return mix

    def swa_heads(band_start, q_pos0):
        outs = []
        k_ctx = kc_ref[0:CTX_LEN, :]
        v_ctx = vc_ref[0:CTX_LEN, :]
        if band_start is not None:
            k_band = kc_ref[pl.ds(band_start, BAND), :]
            v_band = vc_ref[pl.ds(band_start, BAND), :]
            qi = lax.broadcasted_iota(jnp.int32, (2 * TM, BAND), 0) & (TM - 1)
            kj = lax.broadcasted_iota(jnp.int32, (2 * TM, BAND), 1)
            dist = qi - kj + (q_pos0 - (band_start - CTX_LEN))
            valid = jnp.abs(dist) <= WINDOW
        for kv in range(C_KV_HEADS):
            qq = jnp.concatenate([qc_ref[:, (2 * kv) * LANES:(2 * kv + 1) * LANES],
                                  qc_ref[:, (2 * kv + 1) * LANES:(2 * kv + 2) * LANES]], axis=0)
            sink = jnp.where(row2 < TM, sink_ref[0:1, 2 * kv:2 * kv + 1], sink_ref[0:1, 2 * kv + 1:2 * kv + 2])
            s_ctx = _dot_nt(qq, k_ctx)
            m = jnp.maximum(jnp.max(s_ctx, axis=-1, keepdims=True), sink)
            if band_start is not None:
                s_band = jnp.where(valid, _dot_nt(qq, k_band), NEG_INF)
                m = jnp.maximum(m, jnp.max(s_band, axis=-1, keepdims=True))
                p_band = jnp.exp(s_band - m)
            p_ctx = jnp.exp(s_ctx - m)
            denom = jnp.sum(p_ctx, axis=-1, keepdims=True) + jnp.exp(sink - m)
            if band_start is not None:
                denom = denom + jnp.sum(p_band, axis=-1, keepdims=True)
            o = _dot(p_ctx.astype(BF16), v_ctx)
            if band_start is not None:
                o = o + _dot(p_band.astype(BF16), v_band)
            o = o * (1.0 / denom)
            o_g0, o_g1 = o[:TM], o[TM:]
            if kv == 0:
                blk = jnp.where(low, o_g0, pltpu.roll(o_g1, LANES // 2, 1))
            else:
                blk = jnp.where(low, pltpu.roll(o_g0, LANES // 2, 1), o_g1)
            outs.append(blk.astype(BF16))
        return outs

    def tile(n_keys, band_start, q_pos0):
        oc = swa_heads(band_start, q_pos0)
        mix = _dot(oa_ref[...], wout_ref[0:A_WIDTH, :])
        mix = mix + _dot(jnp.concatenate(oc, axis=-1), wout_ref[A_WIDTH + B_WIDTH:D_MIX, :])
        mix = diff_heads(n_keys, mix)
        gate = mod_ref[5:6, :]
        o_ref[...] = s_ref[...] + gate * _rms(mix, gpost_ref[...])

    def latent_tile():
        q_pos0 = (t - 1) * TM
        band_l = jnp.clip(q_pos0 - WINDOW, 0, SEQ - BAND)
        band_start = pl.multiple_of(band_l + CTX_LEN, LANES)
        tile(T_ALL, band_start, q_pos0)

    def context_tile():
        tile(CTX_LEN, None, None)

    if t_off == 0:
        pl.when(t == 0)(context_tile)
        pl.when(t > 0)(latent_tile)
    else:
        latent_tile()


def _mix_attn(s, oa, qb, qc, kb, vb, kc, vc, mod, layer, g_post, subln, lam_p, sink, w_out, *, t_off, lam_init):
    n_t = N_TILES - t_off
    tok = lambda width: pl.BlockSpec((None, TM, width), lambda b, t: (b, t + t_off, 0))
    slab = lambda width: pl.BlockSpec((None, T_ALL, width), lambda b, t: (b, 0, 0))
    return pl.pallas_call(
        functools.partial(_mix_attn_kernel, t_off=t_off, lam_init=lam_init),
        grid=(BATCH, n_t),
        in_specs=[
            tok(D_MODEL), tok(A_WIDTH), tok(2 * B_WIDTH), tok(2 * C_WIDTH),
            slab(B_WIDTH), slab(B_WIDTH), slab(LANES), slab(LANES),
            _mod_spec(layer, t_off),
            _const_spec((1, D_MODEL)),
            _const_spec((1, B_V_DIM)),
            _const_spec((4, B_QK_DIM)),
            _const_spec((1, C_HEADS)),
            _const_spec((D_MIX, D_MODEL)),
        ],
        out_specs=pl.BlockSpec((None, TM, D_MODEL), lambda b, t: (b, t, 0)),
        out_shape=jax.ShapeDtypeStruct((BATCH, n_t * TM, D_MODEL), F32),
        scratch_shapes=([pltpu.VMEM((2 * TM, T_ALL), F32)] * 2
                        + [pltpu.VMEM((2 * TM, T_ALL), BF16)] * 2),
        compiler_params=_cparams(2),
        name="mix_attn",
    )(s, oa, qb, qc, kb, vb, kc, vc, mod, g_post.reshape(1, D_MODEL), subln.reshape(1, B_V_DIM), lam_p,
      sink.reshape(1, C_HEADS), w_out)


def _rope_tables():
    rows = SEQ // GRID_W
    row = jnp.repeat(jnp.arange(rows, dtype=F32), GRID_W)
    col = jnp.tile(jnp.arange(GRID_W, dtype=F32), rows)
    quarter = C_HEAD_DIM // 4
    inv = ROPE_BASE ** (-jnp.arange(quarter, dtype=F32) / quarter)
    ar = row[:, None] * inv[None, :]
    ac = col[:, None] * inv[None, :]
    ang = jnp.concatenate([ar, ar, ac, ac] * (LANES // C_HEAD_DIM), axis=-1)
    cos = jnp.concatenate([jnp.ones((CTX_LEN, LANES), F32), jnp.cos(ang)], axis=0)
    sin = jnp.concatenate([jnp.zeros((CTX_LEN, LANES), F32), jnp.sin(ang)], axis=0)
    first = (jnp.arange(LANES) // quarter) % 2 == 0
    sin_a = jnp.where(first[None, :], -sin, 0.0)
    sin_b = jnp.where(first[None, :], 0.0, sin)
    return cos, sin_a, sin_b


def kernel(x, c, ctx, c_ctx, w_mod, b_mod, norm_pre, norm_post, ffn_w_gate, ffn_w_up, ffn_w_down, w_in, w_out,
           gmlp_v_gain, gmlp_w_s, gmlp_b_s, diff_lambda, diff_subln, swa_sink):
    assert x.shape == (BATCH, SEQ, D_MODEL) and ctx.shape == (BATCH, CTX_LEN, D_MODEL)
    c_all = jnp.zeros((MOD_ROWS, D_MODEL), F32).at[:BATCH].set(c).at[CTX_ROW].set(c_ctx)
    mod = _modulation(c_all, w_mod, b_mod).reshape(DEPTH, MOD_ROWS, N_MOD, D_MODEL)

    tables = _rope_tables()
    head_of = jnp.arange(A_WIDTH) // A_HEAD_DIM
    seg = (head_of[:, None] == head_of[None, :]).astype(BF16)
    wg, wu, wd = ffn_w_gate.astype(BF16), ffn_w_up.astype(BF16), ffn_w_down.astype(BF16)
    w_in_b, w_out_b, w_s_b = w_in.astype(BF16), w_out.astype(BF16), gmlp_w_s.astype(BF16)
    b_s = jnp.repeat(jnp.swapaxes(gmlp_b_s, 1, 2), A_HEAD_DIM, axis=2)
    n_pre = norm_pre.reshape(DEPTH, 3, 1, D_MODEL)
    n_post = norm_post.reshape(DEPTH, 3, 1, D_MODEL)

    s = x
    for l in range(DEPTH):
        last = l == DEPTH - 1
        lam_init = 0.8 - 0.6 * math.exp(-0.3 * l)
        s = _ffn(s, mod, l, 0, 0, n_pre, n_post, wg, wu, wd, ctx=ctx if l == 0 else None)
        oa, qb, qc, kb, vb, kc, vc = _mix_in(s, mod, l, norm_pre[l, 1], w_in_b[l], tables, gmlp_v_gain[l],
                                             w_s_b[l], b_s[l], seg)
        s = _mix_attn(s, oa, qb, qc, kb, vb, kc, vc, mod, l, norm_post[l, 1], diff_subln[l], diff_lambda[l],
                      swa_sink[l], w_out_b[l], t_off=1 if last else 0, lam_init=lam_init)
        s = _ffn(s, mod, l, 2, 1, n_pre, n_post, wg, wu, wd)
    return s
```

```python
import functools
import math

import jax
import jax.numpy as jnp
from jax import lax
from jax.experimental import pallas as pl
from jax.experimental.pallas import tpu as pltpu

D_MODEL = 1024
BATCH = 8
SEQ = 2048
DEPTH = 4
GRID_W = 64
CTX_LEN = 256
N_MOD = 9
D_FF = int(math.ceil(8 * D_MODEL / 3 / 128)) * 128
FFN_RES = 0.5
ROPE_BASE = 10000.0
EPS = 1e-6
NEG_INF = -1e30

A_WIDTH = D_MODEL // 4
A_HEADS = 4
A_HEAD_DIM = A_WIDTH // A_HEADS
CHUNK = 128
B_WIDTH = D_MODEL // 2
B_HEADS = 4
B_V_DIM = B_WIDTH // B_HEADS
B_QK_DIM = B_V_DIM // 2
C_WIDTH = D_MODEL // 4
C_HEAD_DIM = 64
C_HEADS = C_WIDTH // C_HEAD_DIM
C_KV_HEADS = 2
C_GROUP = C_HEADS // C_KV_HEADS
WINDOW = 128
D_MIX = A_WIDTH + B_WIDTH + C_WIDTH

OFF_UV = 0
OFF_QB = OFF_UV + 2 * A_WIDTH
OFF_QC = OFF_QB + B_HEADS * 2 * B_QK_DIM
OFF_KB = OFF_QC + C_HEADS * C_HEAD_DIM
OFF_VB = OFF_KB + B_HEADS * 2 * B_QK_DIM
OFF_KC = OFF_VB + B_HEADS * B_V_DIM
OFF_VC = OFF_KC + C_KV_HEADS * C_HEAD_DIM
IN_COLS = OFF_VC + C_KV_HEADS * C_HEAD_DIM

T_ALL = SEQ + CTX_LEN
LANES = 128
TM = 256
N_TILES = T_ALL // TM
LATENT_TILES = SEQ // TM
CTX_TILE = N_TILES - 1
MOD_ROWS = 16
CTX_ROW = BATCH
MOD_TN = 1152
SUB_SLAB = 3
SUB_LATENT = 2
BAND = 2 * WINDOW + TM
ROW_BLK = 16
VMEM_LIMIT = 56 * 1024 * 1024

F32 = jnp.float32
BF16 = jnp.bfloat16


def _cparams(n_axes):
    return pltpu.CompilerParams(dimension_semantics=("arbitrary",) * n_axes, vmem_limit_bytes=VMEM_LIMIT)


def _rms(xf, g):
    ms = jnp.mean(xf * xf, axis=-1, keepdims=True)
    return xf * lax.rsqrt(ms + EPS) * g


def _dot(a, b):
    return jnp.dot(a, b, preferred_element_type=F32)


def _dot_nt(a, b):
    return lax.dot_general(a, b, (((1,), (1,)), ((), ())), preferred_element_type=F32)


def _resident(block, index):
    return pl.BlockSpec(block, lambda b, t: index, pipeline_mode=pl.Buffered(1))


def _mod_specs(layer):
    batch_row = pl.BlockSpec((None, None, N_MOD, D_MODEL), lambda b, t: (layer, b, 0, 0))
    ctx_row = pl.BlockSpec((None, None, N_MOD, D_MODEL), lambda b, t: (layer, CTX_ROW, 0, 0))
    return [batch_row, ctx_row]


def _mod_kernel(c_ref, w_ref, b_ref, o_ref):
    c = c_ref[...]
    sc = c * jax.nn.sigmoid(c)
    o_ref[...] = _dot(sc.astype(BF16), w_ref[...].astype(BF16)) + b_ref[...]


def _modulation(c_all, w_mod, b_mod):
    n_cols = N_MOD * D_MODEL
    return pl.pallas_call(
        _mod_kernel,
        grid=(DEPTH, n_cols // MOD_TN),
        in_specs=[
            pl.BlockSpec((MOD_ROWS, D_MODEL), lambda l, n: (0, 0)),
            pl.BlockSpec((None, D_MODEL, MOD_TN), lambda l, n: (l, 0, n)),
            pl.BlockSpec((None, 1, MOD_TN), lambda l, n: (l, 0, n)),
        ],
        out_specs=pl.BlockSpec((None, MOD_ROWS, MOD_TN), lambda l, n: (l, 0, n)),
        out_shape=jax.ShapeDtypeStruct((DEPTH, MOD_ROWS, n_cols), F32),
        compiler_params=_cparams(2),
        name="modulation",
    )(c_all, w_mod, b_mod.reshape(DEPTH, 1, n_cols))


def _ffn_kernel(*refs, j, n_sub, has_ctx, split_in):
    if split_in:
        x_refs, ctx_ref, rest = refs[:n_sub], refs[n_sub], refs[n_sub + 1:]
    else:
        s_ref, rest = refs[0], refs[1:]
    modb_ref, modc_ref, gpre_ref, gpost_ref, wg_ref, wu_ref, wd_ref, o_ref = rest
    last = pl.program_id(1) == pl.num_programs(1) - 1
    for i in range(n_sub):
        rows = slice(i * TM, (i + 1) * TM)
        ctx_rows = has_ctx and i == n_sub - 1
        if split_in:
            xf = x_refs[i][...]
            if ctx_rows:
                xf = jnp.where(last, ctx_ref[...], xf)
        else:
            xf = s_ref[rows, :]

        def mod_row(k):
            row = modb_ref[k:k + 1, :]
            return jnp.where(last, modc_ref[k:k + 1, :], row) if ctx_rows else row

        shift, scale, gate = mod_row(3 * j), mod_row(3 * j + 1), mod_row(3 * j + 2)
        y = (_rms(xf, gpre_ref[...]) * (1.0 + scale) + shift).astype(BF16)
        g = _dot(y, wg_ref[...])
        u = _dot(y, wu_ref[...])
        h = (g * jax.nn.sigmoid(g) * u).astype(BF16)
        o = _dot(h, wd_ref[...])
        o_ref[rows, :] = xf + (FFN_RES * gate) * _rms(o, gpost_ref[...])


def _ffn(s, mod, layer, j, jj, norm_pre, norm_post, wg, wu, wd, ctx=None):
    split_in = ctx is not None
    rows = T_ALL if split_in else s.shape[1]
    has_ctx = rows == T_ALL
    n_sub = SUB_SLAB if has_ctx else SUB_LATENT
    tmf = n_sub * TM
    tile = pl.BlockSpec((None, tmf, D_MODEL), lambda b, t: (b, t, 0))
    if split_in:
        def latent_tile(i):
            return pl.BlockSpec((None, TM, D_MODEL),
                                lambda b, t: (b, jnp.minimum(n_sub * t + i, LATENT_TILES - 1), 0))
        data_specs = [latent_tile(i) for i in range(n_sub)]
        data_specs.append(pl.BlockSpec((None, TM, D_MODEL), lambda b, t: (b, 0, 0)))
        data = [s] * n_sub + [ctx]
    else:
        data_specs, data = [tile], [s]
    return pl.pallas_call(
        functools.partial(_ffn_kernel, j=j, n_sub=n_sub, has_ctx=has_ctx, split_in=split_in),
        grid=(BATCH, rows // tmf),
        in_specs=data_specs + _mod_specs(layer) + [
            _resident((None, None, 1, D_MODEL), (layer, j, 0, 0)),
            _resident((None, None, 1, D_MODEL), (layer, j, 0, 0)),
            _resident((None, None, D_MODEL, D_FF), (layer, jj, 0, 0)),
            _resident((None, None, D_MODEL, D_FF), (layer, jj, 0, 0)),
            _resident((None, None, D_FF, D_MODEL), (layer, jj, 0, 0)),
        ],
        out_specs=tile,
        out_shape=jax.ShapeDtypeStruct((BATCH, rows, D_MODEL), F32),
        compiler_params=_cparams(2),
        name="ffn",
    )(*data, mod, mod, norm_pre, norm_post, wg, wu, wd)


def _mix_in_kernel(s_ref, modb_ref, modc_ref, gpre_ref, win_ref, cos_ref, sina_ref, sinb_ref, vgain_ref, ws_ref,
                   bs_ref, seg_ref, oa_ref, qb_ref, qc_ref, kb_ref, vb_ref, kc_ref, vc_ref, *, n_sub):
    last = pl.program_id(1) == pl.num_programs(1) - 1
    quarter = C_HEAD_DIM // 4
    lane = lax.broadcasted_iota(jnp.int32, (TM, LANES), 1)
    low = lane < (LANES // 2)
    a_lane = lax.broadcasted_iota(jnp.int32, (CHUNK, A_WIDTH), 1)
    q_scale = B_QK_DIM ** -0.5
    c_scale = C_HEAD_DIM ** -0.5

    for i in range(n_sub):
        rows = slice(i * TM, (i + 1) * TM)
        ctx_rows = i == n_sub - 1

        def mod_row(k):
            row = modb_ref[k:k + 1, :]
            return jnp.where(last, modc_ref[k:k + 1, :], row) if ctx_rows else row

        ax = (_rms(s_ref[rows, :], gpre_ref[...]) * (1.0 + mod_row(4)) + mod_row(3)).astype(BF16)
        proj = _dot(ax, win_ref[...])

        cos = cos_ref[rows, :]
        sin_a = sina_ref[rows, :]
        sin_b = sinb_ref[rows, :]

        def rope(xb):
            return xb * cos + pltpu.roll(xb, LANES - quarter, 1) * sin_a + pltpu.roll(xb, quarter, 1) * sin_b

        def block(off, n):
            return proj[:, off + n * LANES:off + (n + 1) * LANES]

        for h in range(B_HEADS):
            q = rope(block(OFF_QB, h)) * q_scale
            qb_ref[rows, (2 * h) * LANES:(2 * h + 1) * LANES] = jnp.where(low, q, 0.0).astype(BF16)
            qb_ref[rows, (2 * h + 1) * LANES:(2 * h + 2) * LANES] = jnp.where(low, 0.0, q).astype(BF16)
            kb_ref[rows, h * LANES:(h + 1) * LANES] = rope(block(OFF_KB, h)).astype(BF16)
        vb_ref[rows, :] = proj[:, OFF_VB:OFF_VB + B_WIDTH].astype(BF16)

        for kv in range(C_KV_HEADS):
            q = rope(block(OFF_QC, kv)) * c_scale
            q_sw = pltpu.roll(q, LANES // 2, 1)
            own_low = kv == 0
            g0 = jnp.where(low, q, 0.0) if own_low else jnp.where(low, 0.0, q_sw)
            g1 = jnp.where(low, q_sw, 0.0) if own_low else jnp.where(low, 0.0, q)
            qc_ref[rows, (2 * kv) * LANES:(2 * kv + 1) * LANES] = g0.astype(BF16)
            qc_ref[rows, (2 * kv + 1) * LANES:(2 * kv + 2) * LANES] = g1.astype(BF16)
        kc_ref[rows, :] = rope(block(OFF_KC, 0)).astype(BF16)
        vc_ref[rows, :] = proj[:, OFF_VC:OFF_VC + LANES].astype(BF16)

        uv = jax.nn.gelu(proj[:, OFF_UV:OFF_UV + 2 * A_WIDTH], approximate=True)
        u = uv[:, :A_WIDTH]
        v = uv[:, A_WIDTH:]
        v2 = v * v
        v2_hi = v2.astype(BF16)
        v2_lo = (v2 - v2_hi.astype(F32)).astype(BF16)
        seg = seg_ref[...]
        ms = (_dot(v2_hi, seg) + _dot(v2_lo, seg)) * (1.0 / A_HEAD_DIM)
        vn = (v * lax.rsqrt(ms + EPS) * vgain_ref[...]).astype(BF16)
        for c in range(TM // CHUNK):
            vc = vn[c * CHUNK:(c + 1) * CHUNK, :]
            mixed = _dot(ws_ref[A_HEADS - 1], vc)
            for h in range(A_HEADS - 2, -1, -1):
                mixed = jnp.where(a_lane < (h + 1) * A_HEAD_DIM, _dot(ws_ref[h], vc), mixed)
            mixed = mixed + bs_ref[...]
            out_rows = slice(i * TM + c * CHUNK, i * TM + (c + 1) * CHUNK)
            oa_ref[out_rows, :] = (u[c * CHUNK:(c + 1) * CHUNK, :] * mixed).astype(BF16)


def _mix_in(s, mod, layer, norm_pre, w_in, tables, v_gain, w_s, b_s, seg):
    cos_t, sin_a, sin_b = tables
    n_sub = SUB_SLAB
    tmi = n_sub * TM
    tok = lambda width: pl.BlockSpec((None, tmi, width), lambda b, t: (b, t, 0))
    pos = pl.BlockSpec((tmi, LANES), lambda b, t: (t, 0))
    out_widths = (A_WIDTH, 2 * B_WIDTH, 2 * C_WIDTH, B_WIDTH, B_WIDTH, LANES, LANES)
    return pl.pallas_call(
        functools.partial(_mix_in_kernel, n_sub=n_sub),
        grid=(BATCH, T_ALL // tmi),
        in_specs=[tok(D_MODEL)] + _mod_specs(layer) + [
            _resident((None, None, 1, D_MODEL), (layer, 1, 0, 0)),
            _resident((None, D_MODEL, IN_COLS), (layer, 0, 0)),
            pos, pos, pos,
            _resident((None, 1, A_WIDTH), (layer, 0, 0)),
            _resident((None, A_HEADS, CHUNK, CHUNK), (layer, 0, 0, 0)),
            _resident((None, CHUNK, A_WIDTH), (layer, 0, 0)),
            _resident((A_WIDTH, A_WIDTH), (0, 0)),
        ],
        out_specs=[tok(w) for w in out_widths],
        out_shape=[jax.ShapeDtypeStruct((BATCH, T_ALL, w), BF16) for w in out_widths],
        compiler_params=_cparams(2),
        name="mix_in",
    )(s, mod, mod, norm_pre, w_in, cos_t, sin_a, sin_b, v_gain, w_s, b_s, seg)


def _mix_attn_kernel(s_ref, oa_ref, qb_ref, qc_ref, kb_ref, vb_ref, kc_ref, vc_ref, mod_ref, gpost_ref,
                     subln_ref, lamp_ref, sink_ref, wout_ref, o_ref, *scratch, n_sub, ctx_mode, lam_init):
    n_keys = CTX_LEN if ctx_mode else T_ALL
    s_scrs, p_scrs = scratch[:n_sub], scratch[n_sub:]
    t = pl.program_id(1)

    lam_p = lamp_ref[...]
    lam = (jnp.exp(jnp.sum(lam_p[0:1, :] * lam_p[1:2, :], axis=-1, keepdims=True))
           - jnp.exp(jnp.sum(lam_p[2:3, :] * lam_p[3:4, :], axis=-1, keepdims=True)) + lam_init)

    row2 = lax.broadcasted_iota(jnp.int32, (2 * TM, 1), 0)
    lane = lax.broadcasted_iota(jnp.int32, (TM, LANES), 1)
    low = lane < (LANES // 2)
    pair = 2 * LANES

    def tile_rows(i):
        return slice(i * TM, (i + 1) * TM)

    def swa_heads(i):
        rows = tile_rows(i)
        if ctx_mode:
            k_all, v_all = kc_ref[...], vc_ref[...]
        else:
            q_pos0 = (t * n_sub + i) * TM
            band_start = pl.multiple_of(jnp.clip(q_pos0 - WINDOW, 0, SEQ - BAND), LANES)
            k_all = jnp.concatenate([kc_ref[pl.ds(band_start, BAND), :], kc_ref[SEQ:T_ALL, :]], axis=0)
            v_all = jnp.concatenate([vc_ref[pl.ds(band_start, BAND), :], vc_ref[SEQ:T_ALL, :]], axis=0)
            qi = lax.broadcasted_iota(jnp.int32, (TM, BAND + CTX_LEN), 0)
            kj = lax.broadcasted_iota(jnp.int32, (TM, BAND + CTX_LEN), 1)
            dist = jnp.where(kj >= BAND, 0, qi - kj + (q_pos0 - band_start))
            valid = jnp.abs(dist) <= WINDOW
        outs = []
        for kv in range(C_KV_HEADS):
            qq = jnp.concatenate([qc_ref[rows, (2 * kv) * LANES:(2 * kv + 1) * LANES],
                                  qc_ref[rows, (2 * kv + 1) * LANES:(2 * kv + 2) * LANES]], axis=0)
            sink = jnp.where(row2 < TM, sink_ref[0:1, 2 * kv:2 * kv + 1], sink_ref[0:1, 2 * kv + 1:2 * kv + 2])
            s = _dot_nt(qq, k_all)
            if not ctx_mode:
                s = jnp.concatenate([jnp.where(valid, s[:TM], NEG_INF), jnp.where(valid, s[TM:], NEG_INF)], axis=0)
            m = jnp.maximum(jnp.max(s, axis=-1, keepdims=True), sink)
            p = jnp.exp(s - m)
            denom = jnp.sum(p, axis=-1, keepdims=True) + jnp.exp(sink - m)
            o = _dot(p.astype(BF16), v_all) * (1.0 / denom)
            o_g0, o_g1 = o[:TM], o[TM:]
            if kv == 0:
                blk = jnp.where(low, o_g0, pltpu.roll(o_g1, LANES // 2, 1))
            else:
                blk = jnp.where(low, pltpu.roll(o_g0, LANES // 2, 1), o_g1)
            outs.append(blk.astype(BF16))
        return outs

    def scores(i, h):
        rows = tile_rows(i)
        qq = jnp.concatenate([qb_ref[rows, (2 * h) * LANES:(2 * h + 1) * LANES],
                              qb_ref[rows, (2 * h + 1) * LANES:(2 * h + 2) * LANES]], axis=0)
        s_scrs[i][:, 0:n_keys] = _dot_nt(qq, kb_ref[:, h * LANES:(h + 1) * LANES])

    def diff_head(i, h):
        s_scr, p_scr = s_scrs[i], p_scrs[2 * i + h % 2]
        m = jnp.max(s_scr[:, 0:n_keys], axis=-1, keepdims=True)
        l_parts = []
        for rb in range(2 * TM // ROW_BLK):
            rows = slice(rb * ROW_BLK, (rb + 1) * ROW_BLK)
            m_blk = m[rows]
            acc = None
            for c in range(n_keys // LANES):
                cols = slice(c * LANES, (c + 1) * LANES)
                e = jnp.exp(s_scr[rows, cols] - m_blk)
                acc = e if acc is None else acc + e
                p_scr[rows, cols] = e.astype(BF16)
            l_parts.append(acc)
        r = 1.0 / jnp.sum(jnp.concatenate(l_parts, axis=0), axis=-1, keepdims=True)
        pv = _dot(p_scr[:, 0:n_keys], vb_ref[:, h * LANES:(h + 1) * LANES])
        o = pv[:TM] * r[:TM] - pv[TM:] * (lam * r[TM:])
        return (_rms(o, subln_ref[...]) * (1.0 - lam_init)).astype(BF16)

    mix = []
    for i in range(n_sub):
        oc = swa_heads(i)
        part = _dot(oa_ref[tile_rows(i), :], wout_ref[0:A_WIDTH, :])
        mix.append(part + _dot(jnp.concatenate(oc, axis=-1), wout_ref[A_WIDTH + B_WIDTH:D_MIX, :]))
    for i in range(n_sub):
        scores(i, 0)
    ob = [[] for _ in range(n_sub)]
    for h in range(B_HEADS):
        for i in range(n_sub):
            ob[i].append(diff_head(i, h))
            if h + 1 < B_HEADS:
                scores(i, h + 1)
            if h % 2 == 1:
                w_rows = slice(A_WIDTH + (h // 2) * pair, A_WIDTH + (h // 2 + 1) * pair)
                mix[i] = mix[i] + _dot(jnp.concatenate(ob[i][h - 1:h + 1], axis=-1), wout_ref[w_rows, :])
    gate = mod_ref[5:6, :]
    for i in range(n_sub):
        rows = tile_rows(i)
        o_ref[rows, :] = s_ref[rows, :] + gate * _rms(mix[i], gpost_ref[...])


def _mix_attn(s, oa, qb, qc, kb, vb, kc, vc, mod, layer, norm_post, subln, lam_p, sink, w_out, *, ctx_mode, lam_init):
    n_sub = 1 if ctx_mode else SUB_LATENT
    tma = n_sub * TM
    out_rows = CTX_LEN if ctx_mode else SEQ
    n_keys = CTX_LEN if ctx_mode else T_ALL
    t0 = CTX_TILE if ctx_mode else 0
    tok = lambda width: pl.BlockSpec((None, tma, width), lambda b, t: (b, t + t0, 0))
    if ctx_mode:
        keys = lambda width: pl.BlockSpec((None, CTX_LEN, width), lambda b, t: (b, CTX_TILE, 0))
        mod_spec = pl.BlockSpec((None, None, N_MOD, D_MODEL), lambda b, t: (layer, CTX_ROW, 0, 0))
    else:
        keys = lambda width: pl.BlockSpec((None, T_ALL, width), lambda b, t: (b, 0, 0))
        mod_spec = pl.BlockSpec((None, None, N_MOD, D_MODEL), lambda b, t: (layer, b, 0, 0))
    return pl.pallas_call(
        functools.partial(_mix_attn_kernel, n_sub=n_sub, ctx_mode=ctx_mode, lam_init=lam_init),
        grid=(BATCH, out_rows // tma),
        in_specs=[
            tok(D_MODEL), tok(A_WIDTH), tok(2 * B_WIDTH), tok(2 * C_WIDTH),
            keys(B_WIDTH), keys(B_WIDTH), keys(LANES), keys(LANES),
            mod_spec,
            _resident((None, None, 1, D_MODEL), (layer, 1, 0, 0)),
            _resident((None, 1, B_V_DIM), (layer, 0, 0)),
            _resident((None, 4, B_QK_DIM), (layer, 0, 0)),
            _resident((None, 1, C_HEADS), (layer, 0, 0)),
            _resident((None, D_MIX, D_MODEL), (layer, 0, 0)),
        ],
        out_specs=pl.BlockSpec((None, tma, D_MODEL), lambda b, t: (b, t, 0)),
        out_shape=jax.ShapeDtypeStruct((BATCH, out_rows, D_MODEL), F32),
        scratch_shapes=([pltpu.VMEM((2 * TM, n_keys), F32)] * n_sub
                        + [pltpu.VMEM((2 * TM, n_keys), BF16)] * (2 * n_sub)),
        compiler_params=_cparams(2),
        name="mix_attn_ctx" if ctx_mode else "mix_attn",
    )(s, oa, qb, qc, kb, vb, kc, vc, mod, norm_post, subln, lam_p, sink, w_out)


def _rope_tables():
    rows = SEQ // GRID_W
    row = jnp.repeat(jnp.arange(rows, dtype=F32), GRID_W)
    col = jnp.tile(jnp.arange(GRID_W, dtype=F32), rows)
    quarter = C_HEAD_DIM // 4
    inv = ROPE_BASE ** (-jnp.arange(quarter, dtype=F32) / quarter)
    ar = row[:, None] * inv[None, :]
    ac = col[:, None] * inv[None, :]
    ang = jnp.concatenate([ar, ar, ac, ac] * (LANES // C_HEAD_DIM), axis=-1)
    cos = jnp.concatenate([jnp.cos(ang), jnp.ones((CTX_LEN, LANES), F32)], axis=0)
    sin = jnp.concatenate([jnp.sin(ang), jnp.zeros((CTX_LEN, LANES), F32)], axis=0)
    first = (jnp.arange(LANES) // quarter) % 2 == 0
    sin_a = jnp.where(first[None, :], -sin, 0.0)
    sin_b = jnp.where(first[None, :], 0.0, sin)
    return cos, sin_a, sin_b


def kernel(x, c, ctx, c_ctx, w_mod, b_mod, norm_pre, norm_post, ffn_w_gate, ffn_w_up, ffn_w_down, w_in, w_out,
           gmlp_v_gain, gmlp_w_s, gmlp_b_s, diff_lambda, diff_subln, swa_sink):
    assert x.shape == (BATCH, SEQ, D_MODEL) and ctx.shape == (BATCH, CTX_LEN, D_MODEL)
    c_all = jnp.zeros((MOD_ROWS, D_MODEL), F32).at[:BATCH].set(c).at[CTX_ROW].set(c_ctx)
    mod = _modulation(c_all, w_mod, b_mod).reshape(DEPTH, MOD_ROWS, N_MOD, D_MODEL)

    tables = _rope_tables()
    head_of = jnp.arange(A_WIDTH) // A_HEAD_DIM
    seg = (head_of[:, None] == head_of[None, :]).astype(BF16)
    wg, wu, wd = ffn_w_gate.astype(BF16), ffn_w_up.astype(BF16), ffn_w_down.astype(BF16)
    w_in_b, w_out_b, w_s_b = w_in.astype(BF16), w_out.astype(BF16), gmlp_w_s.astype(BF16)
    b_s = jnp.repeat(jnp.swapaxes(gmlp_b_s, 1, 2), A_HEAD_DIM, axis=2)
    n_pre = norm_pre.reshape(DEPTH, 3, 1, D_MODEL)
    n_post = norm_post.reshape(DEPTH, 3, 1, D_MODEL)
    v_gain = gmlp_v_gain.reshape(DEPTH, 1, A_WIDTH)
    subln = diff_subln.reshape(DEPTH, 1, B_V_DIM)
    sink = swa_sink.reshape(DEPTH, 1, C_HEADS)

    s = _ffn(x, mod, 0, 0, 0, n_pre, n_post, wg, wu, wd, ctx=ctx)
    for l in range(DEPTH):
        lam_init = 0.8 - 0.6 * math.exp(-0.3 * l)
        if l > 0:
            s = _ffn(s, mod, l, 0, 0, n_pre, n_post, wg, wu, wd)
        proj = _mix_in(s, mod, l, n_pre, w_in_b, tables, v_gain, w_s_b, b_s, seg)
        attn = functools.partial(_mix_attn, s, *proj, mod, l, n_post, subln, diff_lambda, sink, w_out_b,
                                 lam_init=lam_init)
        latent = attn(ctx_mode=False)
        if l == DEPTH - 1:
            return _ffn(latent, mod, l, 2, 1, n_pre, n_post, wg, wu, wd)
        s = _ffn(latent, mod, l, 2, 1, n_pre, n_post, wg, wu, wd, ctx=attn(ctx_mode=True))
```

```python
import functools
import math

import jax
import jax.numpy as jnp
from jax import lax
from jax.experimental import pallas as pl
from jax.experimental.pallas import tpu as pltpu

D_MODEL = 1024
BATCH = 8
SEQ = 2048
DEPTH = 4
GRID_W = 64
CTX_LEN = 256
N_MOD = 9
D_FF = int(math.ceil(8 * D_MODEL / 3 / 128)) * 128
FFN_RES = 0.5
ROPE_BASE = 10000.0
EPS = 1e-6
NEG_INF = -1e30

A_WIDTH = D_MODEL // 4
A_HEADS = 4
A_HEAD_DIM = A_WIDTH // A_HEADS
CHUNK = 128
B_WIDTH = D_MODEL // 2
B_HEADS = 4
B_V_DIM = B_WIDTH // B_HEADS
B_QK_DIM = B_V_DIM // 2
C_WIDTH = D_MODEL // 4
C_HEAD_DIM = 64
C_HEADS = C_WIDTH // C_HEAD_DIM
C_KV_HEADS = 2
C_GROUP = C_HEADS // C_KV_HEADS
WINDOW = 128
D_MIX = A_WIDTH + B_WIDTH + C_WIDTH

OFF_UV = 0
OFF_QB = OFF_UV + 2 * A_WIDTH
OFF_QC = OFF_QB + B_HEADS * 2 * B_QK_DIM
OFF_KB = OFF_QC + C_HEADS * C_HEAD_DIM
OFF_VB = OFF_KB + B_HEADS * 2 * B_QK_DIM
OFF_KC = OFF_VB + B_HEADS * B_V_DIM
OFF_VC = OFF_KC + C_KV_HEADS * C_HEAD_DIM
IN_COLS = OFF_VC + C_KV_HEADS * C_HEAD_DIM

T_ALL = SEQ + CTX_LEN
LANES = 128
TM = 256
N_TILES = T_ALL // TM
LATENT_TILES = SEQ // TM
CTX_TILE = N_TILES - 1
MOD_ROWS = 16
CTX_ROW = BATCH
MOD_TN = 1152
SUB_SLAB = 3
SUB_LATENT = 2
ATT_ROWS = TM
ATT_SUB = SUB_LATENT
LOG2E = math.log2(math.e)
ROW_BLK = 16
VMEM_LIMIT = 56 * 1024 * 1024

F32 = jnp.float32
BF16 = jnp.bfloat16


def _cparams(n_axes):
    return pltpu.CompilerParams(dimension_semantics=("arbitrary",) * n_axes, vmem_limit_bytes=VMEM_LIMIT)


def _rms(xf, g):
    ms = jnp.mean(xf * xf, axis=-1, keepdims=True)
    return xf * lax.rsqrt(ms + EPS) * g


def _dot(a, b):
    return jnp.dot(a, b, preferred_element_type=F32)


def _dot_nt(a, b):
    return lax.dot_general(a, b, (((1,), (1,)), ((), ())), preferred_element_type=F32)


def _resident(block, index):
    return pl.BlockSpec(block, lambda b, t: index, pipeline_mode=pl.Buffered(1))


def _mod_specs(layer):
    batch_row = pl.BlockSpec((None, None, N_MOD, D_MODEL), lambda b, t: (layer, b, 0, 0))
    ctx_row = pl.BlockSpec((None, None, N_MOD, D_MODEL), lambda b, t: (layer, CTX_ROW, 0, 0))
    return [batch_row, ctx_row]


def _mod_kernel(c_ref, w_ref, b_ref, o_ref):
    c = c_ref[...]
    sc = c * jax.nn.sigmoid(c)
    o_ref[...] = _dot(sc.astype(BF16), w_ref[...].astype(BF16)) + b_ref[...]


def _modulation(c_all, w_mod, b_mod):
    n_cols = N_MOD * D_MODEL
    return pl.pallas_call(
        _mod_kernel,
        grid=(DEPTH, n_cols // MOD_TN),
        in_specs=[
            pl.BlockSpec((MOD_ROWS, D_MODEL), lambda l, n: (0, 0)),
            pl.BlockSpec((None, D_MODEL, MOD_TN), lambda l, n: (l, 0, n)),
            pl.BlockSpec((None, 1, MOD_TN), lambda l, n: (l, 0, n)),
        ],
        out_specs=pl.BlockSpec((None, MOD_ROWS, MOD_TN), lambda l, n: (l, 0, n)),
        out_shape=jax.ShapeDtypeStruct((DEPTH, MOD_ROWS, n_cols), F32),
        compiler_params=_cparams(2),
        name="modulation",
    )(c_all, w_mod, b_mod.reshape(DEPTH, 1, n_cols))


def _ffn_kernel(*refs, j, n_sub, has_ctx, split_in):
    if split_in:
        x_refs, ctx_ref, rest = refs[:n_sub], refs[n_sub], refs[n_sub + 1:]
    else:
        s_ref, rest = refs[0], refs[1:]
    modb_ref, modc_ref, gpre_ref, gpost_ref, wg_ref, wu_ref, wd_ref, o_ref = rest
    last = pl.program_id(1) == pl.num_programs(1) - 1
    for i in range(n_sub):
        rows = slice(i * TM, (i + 1) * TM)
        ctx_rows = has_ctx and i == n_sub - 1
        if split_in:
            xf = x_refs[i][...]
            if ctx_rows:
                xf = jnp.where(last, ctx_ref[...], xf)
        else:
            xf = s_ref[rows, :]

        def mod_row(k):
            row = modb_ref[k:k + 1, :]
            return jnp.where(last, modc_ref[k:k + 1, :], row) if ctx_rows else row

        shift, scale, gate = mod_row(3 * j), mod_row(3 * j + 1), mod_row(3 * j + 2)
        y = (_rms(xf, gpre_ref[...]) * (1.0 + scale) + shift).astype(BF16)
        g = _dot(y, wg_ref[...])
        u = _dot(y, wu_ref[...])
        h = (g * jax.nn.sigmoid(g) * u).astype(BF16)
        o = _dot(h, wd_ref[...])
        o_ref[rows, :] = xf + (FFN_RES * gate) * _rms(o, gpost_ref[...])


def _ffn(s, mod, layer, j, jj, norm_pre, norm_post, wg, wu, wd, ctx=None):
    split_in = ctx is not None
    rows = T_ALL if split_in else s.shape[1]
    has_ctx = rows == T_ALL
    n_sub = SUB_SLAB if has_ctx else SUB_LATENT
    tmf = n_sub * TM
    tile = pl.BlockSpec((None, tmf, D_MODEL), lambda b, t: (b, t, 0))
    if split_in:
        def latent_tile(i):
            return pl.BlockSpec((None, TM, D_MODEL),
                                lambda b, t: (b, jnp.minimum(n_sub * t + i, LATENT_TILES - 1), 0))
        data_specs = [latent_tile(i) for i in range(n_sub)]
        data_specs.append(pl.BlockSpec((None, TM, D_MODEL), lambda b, t: (b, 0, 0)))
        data = [s] * n_sub + [ctx]
    else:
        data_specs, data = [tile], [s]
    return pl.pallas_call(
        functools.partial(_ffn_kernel, j=j, n_sub=n_sub, has_ctx=has_ctx, split_in=split_in),
        grid=(BATCH, rows // tmf),
        in_specs=data_specs + _mod_specs(layer) + [
            _resident((None, None, 1, D_MODEL), (layer, j, 0, 0)),
            _resident((None, None, 1, D_MODEL), (layer, j, 0, 0)),
            _resident((None, None, D_MODEL, D_FF), (layer, jj, 0, 0)),
            _resident((None, None, D_MODEL, D_FF), (layer, jj, 0, 0)),
            _resident((None, None, D_FF, D_MODEL), (layer, jj, 0, 0)),
        ],
        out_specs=tile,
        out_shape=jax.ShapeDtypeStruct((BATCH, rows, D_MODEL), F32),
        compiler_params=_cparams(2),
        name="ffn",
    )(*data, mod, mod, norm_pre, norm_post, wg, wu, wd)


def _mix_in_kernel(s_ref, modb_ref, modc_ref, gpre_ref, win_ref, cos_ref, sina_ref, sinb_ref, vgain_ref, ws_ref,
                   bs_ref, seg_ref, oa_ref, qb_ref, qc_ref, kb_ref, vb_ref, kc_ref, vc_ref, *, n_sub):
    last = pl.program_id(1) == pl.num_programs(1) - 1
    quarter = C_HEAD_DIM // 4
    lane = lax.broadcasted_iota(jnp.int32, (TM, LANES), 1)
    low = lane < (LANES // 2)
    a_lane = lax.broadcasted_iota(jnp.int32, (CHUNK, A_WIDTH), 1)
    q_scale = B_QK_DIM ** -0.5 * LOG2E
    c_scale = C_HEAD_DIM ** -0.5 * LOG2E
    ones_blk = jnp.ones((TM, LANES), BF16)

    for i in range(n_sub):
        rows = slice(i * TM, (i + 1) * TM)
        ctx_rows = i == n_sub - 1

        def mod_row(k):
            row = modb_ref[k:k + 1, :]
            return jnp.where(last, modc_ref[k:k + 1, :], row) if ctx_rows else row

        ax = (_rms(s_ref[rows, :], gpre_ref[...]) * (1.0 + mod_row(4)) + mod_row(3)).astype(BF16)
        proj = _dot(ax, win_ref[...])

        cos = cos_ref[rows, :]
        sin_a = sina_ref[rows, :]
        sin_b = sinb_ref[rows, :]

        def rope(xb):
            return xb * cos + pltpu.roll(xb, LANES - quarter, 1) * sin_a + pltpu.roll(xb, quarter, 1) * sin_b

        def block(off, n):
            return proj[:, off + n * LANES:off + (n + 1) * LANES]

        for h in range(B_HEADS):
            q = rope(block(OFF_QB, h)) * q_scale
            qb_ref[rows, (2 * h) * LANES:(2 * h + 1) * LANES] = jnp.where(low, q, 0.0).astype(BF16)
            qb_ref[rows, (2 * h + 1) * LANES:(2 * h + 2) * LANES] = jnp.where(low, 0.0, q).astype(BF16)
            kb_ref[rows, h * LANES:(h + 1) * LANES] = rope(block(OFF_KB, h)).astype(BF16)
            vb_ref[rows, (2 * h) * LANES:(2 * h + 1) * LANES] = block(OFF_VB, h).astype(BF16)
            vb_ref[rows, (2 * h + 1) * LANES:(2 * h + 2) * LANES] = ones_blk

        for kv in range(C_KV_HEADS):
            q = rope(block(OFF_QC, kv)) * c_scale
            q_sw = pltpu.roll(q, LANES // 2, 1)
            own_low = kv == 0
            g0 = jnp.where(low, q, 0.0) if own_low else jnp.where(low, 0.0, q_sw)
            g1 = jnp.where(low, q_sw, 0.0) if own_low else jnp.where(low, 0.0, q)
            qc_ref[rows, (2 * kv) * LANES:(2 * kv + 1) * LANES] = g0.astype(BF16)
            qc_ref[rows, (2 * kv + 1) * LANES:(2 * kv + 2) * LANES] = g1.astype(BF16)
        kc_ref[rows, :] = rope(block(OFF_KC, 0)).astype(BF16)
        vc_ref[rows, :] = proj[:, OFF_VC:OFF_VC + LANES].astype(BF16)

        uv = jax.nn.gelu(proj[:, OFF_UV:OFF_UV + 2 * A_WIDTH], approximate=True)
        u = uv[:, :A_WIDTH]
        v = uv[:, A_WIDTH:]
        v2 = v * v
        v2_hi = v2.astype(BF16)
        v2_lo = (v2 - v2_hi.astype(F32)).astype(BF16)
        seg = seg_ref[...]
        ms = (_dot(v2_hi, seg) + _dot(v2_lo, seg)) * (1.0 / A_HEAD_DIM)
        vn = (v * lax.rsqrt(ms + EPS) * vgain_ref[...]).astype(BF16)
        for c in range(TM // CHUNK):
            vc = vn[c * CHUNK:(c + 1) * CHUNK, :]
            mixed = _dot(ws_ref[A_HEADS - 1], vc)
            for h in range(A_HEADS - 2, -1, -1):
                mixed = jnp.where(a_lane < (h + 1) * A_HEAD_DIM, _dot(ws_ref[h], vc), mixed)
            mixed = mixed + bs_ref[...]
            out_rows = slice(i * TM + c * CHUNK, i * TM + (c + 1) * CHUNK)
            oa_ref[out_rows, :] = (u[c * CHUNK:(c + 1) * CHUNK, :] * mixed).astype(BF16)


def _mix_in(s, mod, layer, norm_pre, w_in, tables, v_gain, w_s, b_s, seg):
    cos_t, sin_a, sin_b = tables
    n_sub = SUB_SLAB
    tmi = n_sub * TM
    tok = lambda width: pl.BlockSpec((None, tmi, width), lambda b, t: (b, t, 0))
    pos = pl.BlockSpec((tmi, LANES), lambda b, t: (t, 0))
    out_widths = (A_WIDTH, 2 * B_WIDTH, 2 * C_WIDTH, B_WIDTH, 2 * B_WIDTH, LANES, LANES)
    return pl.pallas_call(
        functools.partial(_mix_in_kernel, n_sub=n_sub),
        grid=(BATCH, T_ALL // tmi),
        in_specs=[tok(D_MODEL)] + _mod_specs(layer) + [
            _resident((None, None, 1, D_MODEL), (layer, 1, 0, 0)),
            _resident((None, D_MODEL, IN_COLS), (layer, 0, 0)),
            pos, pos, pos,
            _resident((None, 1, A_WIDTH), (layer, 0, 0)),
            _resident((None, A_HEADS, CHUNK, CHUNK), (layer, 0, 0, 0)),
            _resident((None, CHUNK, A_WIDTH), (layer, 0, 0)),
            _resident((A_WIDTH, A_WIDTH), (0, 0)),
        ],
        out_specs=[tok(w) for w in out_widths],
        out_shape=[jax.ShapeDtypeStruct((BATCH, T_ALL, w), BF16) for w in out_widths],
        compiler_params=_cparams(2),
        name="mix_in",
    )(s, mod, mod, norm_pre, w_in, cos_t, sin_a, sin_b, v_gain, w_s, b_s, seg)


def _mix_attn_kernel(s_ref, oa_ref, qb_ref, qc_ref, kb_ref, vb_ref, kc_ref, vc_ref, mod_ref, gpost_ref,
                     subln_ref, lamp_ref, sink_ref, wout_ref, o_ref, *scratch, n_sub, ctx_mode, lam_init):
    TM = ATT_ROWS
    BAND = 2 * WINDOW + TM
    n_keys = CTX_LEN if ctx_mode else T_ALL
    s_scrs, p_scrs = scratch[:n_sub], scratch[n_sub:]
    t = pl.program_id(1)

    lam_p = lamp_ref[...]
    lam = (jnp.exp(jnp.sum(lam_p[0:1, :] * lam_p[1:2, :], axis=-1, keepdims=True))
           - jnp.exp(jnp.sum(lam_p[2:3, :] * lam_p[3:4, :], axis=-1, keepdims=True)) + lam_init)

    row2 = lax.broadcasted_iota(jnp.int32, (2 * TM, 1), 0)
    lane = lax.broadcasted_iota(jnp.int32, (TM, LANES), 1)
    low = lane < (LANES // 2)
    pair = 2 * LANES

    def tile_rows(i):
        return slice(i * TM, (i + 1) * TM)

    def swa_heads(i):
        rows = tile_rows(i)
        if ctx_mode:
            k_all, v_all = kc_ref[...], vc_ref[...]
        else:
            q_pos0 = (t * n_sub + i) * TM
            band_start = pl.multiple_of(jnp.clip(q_pos0 - WINDOW, 0, SEQ - BAND), LANES)
            k_all = jnp.concatenate([kc_ref[pl.ds(band_start, BAND), :], kc_ref[SEQ:T_ALL, :]], axis=0)
            v_all = jnp.concatenate([vc_ref[pl.ds(band_start, BAND), :], vc_ref[SEQ:T_ALL, :]], axis=0)
            qi = lax.broadcasted_iota(jnp.int32, (TM, BAND + CTX_LEN), 0)
            kj = lax.broadcasted_iota(jnp.int32, (TM, BAND + CTX_LEN), 1)
            dist = jnp.where(kj >= BAND, 0, qi - kj + (q_pos0 - band_start))
            valid = jnp.abs(dist) <= WINDOW
        outs = []
        for kv in range(C_KV_HEADS):
            qq = jnp.concatenate([qc_ref[rows, (2 * kv) * LANES:(2 * kv + 1) * LANES],
                                  qc_ref[rows, (2 * kv + 1) * LANES:(2 * kv + 2) * LANES]], axis=0)
            sink = jnp.where(row2 < TM, sink_ref[0:1, 2 * kv:2 * kv + 1],
                             sink_ref[0:1, 2 * kv + 1:2 * kv + 2]) * LOG2E
            s = _dot_nt(qq, k_all)
            if not ctx_mode:
                s = jnp.concatenate([jnp.where(valid, s[:TM], NEG_INF), jnp.where(valid, s[TM:], NEG_INF)], axis=0)
            m = jnp.maximum(jnp.max(s, axis=-1, keepdims=True), sink)
            p = jnp.exp2(s - m)
            denom = jnp.sum(p, axis=-1, keepdims=True) + jnp.exp2(sink - m)
            o = _dot(p.astype(BF16), v_all) * (1.0 / denom)
            o_g0, o_g1 = o[:TM], o[TM:]
            if kv == 0:
                blk = jnp.where(low, o_g0, pltpu.roll(o_g1, LANES // 2, 1))
            else:
                blk = jnp.where(low, pltpu.roll(o_g0, LANES // 2, 1), o_g1)
            outs.append(blk.astype(BF16))
        return outs

    def scores(i, h):
        rows = tile_rows(i)
        qq = jnp.concatenate([qb_ref[rows, (2 * h) * LANES:(2 * h + 1) * LANES],
                              qb_ref[rows, (2 * h + 1) * LANES:(2 * h + 2) * LANES]], axis=0)
        s_scrs[i][:, 0:n_keys] = _dot_nt(qq, kb_ref[:, h * LANES:(h + 1) * LANES])

    def diff_head(i, h):
        s_scr, p_scr = s_scrs[i], p_scrs[2 * i + h % 2]
        m = jnp.max(s_scr[:, 0:n_keys], axis=-1, keepdims=True)
        for rb in range(2 * TM // ROW_BLK):
            rows = slice(rb * ROW_BLK, (rb + 1) * ROW_BLK)
            m_blk = m[rows]
            for c in range(n_keys // LANES):
                cols = slice(c * LANES, (c + 1) * LANES)
                p_scr[rows, cols] = jnp.exp2(s_scr[rows, cols] - m_blk).astype(BF16)
        pv = _dot(p_scr[:, 0:n_keys], vb_ref[:, (2 * h) * LANES:(2 * h + 2) * LANES])
        num, r = pv[:, 0:LANES], 1.0 / pv[:, LANES:2 * LANES]
        o = num[:TM] * r[:TM] - num[TM:] * (lam * r[TM:])
        return (_rms(o, subln_ref[...]) * (1.0 - lam_init)).astype(BF16)

    mix = []
    for i in range(n_sub):
        oc = swa_heads(i)
        part = _dot(oa_ref[tile_rows(i), :], wout_ref[0:A_WIDTH, :])
        mix.append(part + _dot(jnp.concatenate(oc, axis=-1), wout_ref[A_WIDTH + B_WIDTH:D_MIX, :]))
    for i in range(n_sub):
        scores(i, 0)
    ob = [[] for _ in range(n_sub)]
    for h in range(B_HEADS):
        for i in range(n_sub):
            ob[i].append(diff_head(i, h))
            if h + 1 < B_HEADS:
                scores(i, h + 1)
            if h % 2 == 1:
                w_rows = slice(A_WIDTH + (h // 2) * pair, A_WIDTH + (h // 2 + 1) * pair)
                mix[i] = mix[i] + _dot(jnp.concatenate(ob[i][h - 1:h + 1], axis=-1), wout_ref[w_rows, :])
    gate = mod_ref[5:6, :]
    for i in range(n_sub):
        rows = tile_rows(i)
        o_ref[rows, :] = s_ref[rows, :] + gate * _rms(mix[i], gpost_ref[...])


def _mix_attn(s, oa, qb, qc, kb, vb, kc, vc, mod, layer, norm_post, subln, lam_p, sink, w_out, *, ctx_mode, lam_init):
    n_sub = CTX_LEN // ATT_ROWS if ctx_mode else ATT_SUB
    tma = n_sub * ATT_ROWS
    out_rows = CTX_LEN if ctx_mode else SEQ
    n_keys = CTX_LEN if ctx_mode else T_ALL
    t0 = SEQ // tma if ctx_mode else 0
    tok = lambda width: pl.BlockSpec((None, tma, width), lambda b, t: (b, t + t0, 0))
    if ctx_mode:
        keys = lambda width: pl.BlockSpec((None, CTX_LEN, width), lambda b, t: (b, CTX_TILE, 0))
        mod_spec = pl.BlockSpec((None, None, N_MOD, D_MODEL), lambda b, t: (layer, CTX_ROW, 0, 0))
    else:
        keys = lambda width: pl.BlockSpec((None, T_ALL, width), lambda b, t: (b, 0, 0))
        mod_spec = pl.BlockSpec((None, None, N_MOD, D_MODEL), lambda b, t: (layer, b, 0, 0))
    return pl.pallas_call(
        functools.partial(_mix_attn_kernel, n_sub=n_sub, ctx_mode=ctx_mode, lam_init=lam_init),
        grid=(BATCH, out_rows // tma),
        in_specs=[
            tok(D_MODEL), tok(A_WIDTH), tok(2 * B_WIDTH), tok(2 * C_WIDTH),
            keys(B_WIDTH), keys(2 * B_WIDTH), keys(LANES), keys(LANES),
            mod_spec,
            _resident((None, None, 1, D_MODEL), (layer, 1, 0, 0)),
            _resident((None, 1, B_V_DIM), (layer, 0, 0)),
            _resident((None, 4, B_QK_DIM), (layer, 0, 0)),
            _resident((None, 1, C_HEADS), (layer, 0, 0)),
            _resident((None, D_MIX, D_MODEL), (layer, 0, 0)),
        ],
        out_specs=pl.BlockSpec((None, tma, D_MODEL), lambda b, t: (b, t, 0)),
        out_shape=jax.ShapeDtypeStruct((BATCH, out_rows, D_MODEL), F32),
        scratch_shapes=([pltpu.VMEM((2 * ATT_ROWS, n_keys), F32)] * n_sub
                        + [pltpu.VMEM((2 * ATT_ROWS, n_keys), BF16)] * (2 * n_sub)),
        compiler_params=_cparams(2),
        name="mix_attn_ctx" if ctx_mode else "mix_attn",
    )(s, oa, qb, qc, kb, vb, kc, vc, mod, norm_post, subln, lam_p, sink, w_out)


def _rope_tables():
    rows = SEQ // GRID_W
    row = jnp.repeat(jnp.arange(rows, dtype=F32), GRID_W)
    col = jnp.tile(jnp.arange(GRID_W, dtype=F32), rows)
    quarter = C_HEAD_DIM // 4
    inv = ROPE_BASE ** (-jnp.arange(quarter, dtype=F32) / quarter)
    ar = row[:, None] * inv[None, :]
    ac = col[:, None] * inv[None, :]
    ang = jnp.concatenate([ar, ar, ac, ac] * (LANES // C_HEAD_DIM), axis=-1)
    cos = jnp.concatenate([jnp.cos(ang), jnp.ones((CTX_LEN, LANES), F32)], axis=0)
    sin = jnp.concatenate([jnp.sin(ang), jnp.zeros((CTX_LEN, LANES), F32)], axis=0)
    first = (jnp.arange(LANES) // quarter) % 2 == 0
    sin_a = jnp.where(first[None, :], -sin, 0.0)
    sin_b = jnp.where(first[None, :], 0.0, sin)
    return cos, sin_a, sin_b


def kernel(x, c, ctx, c_ctx, w_mod, b_mod, norm_pre, norm_post, ffn_w_gate, ffn_w_up, ffn_w_down, w_in, w_out,
           gmlp_v_gain, gmlp_w_s, gmlp_b_s, diff_lambda, diff_subln, swa_sink):
    assert x.shape == (BATCH, SEQ, D_MODEL) and ctx.shape == (BATCH, CTX_LEN, D_MODEL)
    c_all = jnp.zeros((MOD_ROWS, D_MODEL), F32).at[:BATCH].set(c).at[CTX_ROW].set(c_ctx)
    mod = _modulation(c_all, w_mod, b_mod).reshape(DEPTH, MOD_ROWS, N_MOD, D_MODEL)

    tables = _rope_tables()
    head_of = jnp.arange(A_WIDTH) // A_HEAD_DIM
    seg = (head_of[:, None] == head_of[None, :]).astype(BF16)
    wg, wu, wd = ffn_w_gate.astype(BF16), ffn_w_up.astype(BF16), ffn_w_down.astype(BF16)
    w_in_b, w_out_b, w_s_b = w_in.astype(BF16), w_out.astype(BF16), gmlp_w_s.astype(BF16)
    b_s = jnp.repeat(jnp.swapaxes(gmlp_b_s, 1, 2), A_HEAD_DIM, axis=2)
    n_pre = norm_pre.reshape(DEPTH, 3, 1, D_MODEL)
    n_post = norm_post.reshape(DEPTH, 3, 1, D_MODEL)
    v_gain = gmlp_v_gain.reshape(DEPTH, 1, A_WIDTH)
    subln = diff_subln.reshape(DEPTH, 1, B_V_DIM)
    sink = swa_sink.reshape(DEPTH, 1, C_HEADS)

    s = _ffn(x, mod, 0, 0, 0, n_pre, n_post, wg, wu, wd, ctx=ctx)
    for l in range(DEPTH):
        lam_init = 0.8 - 0.6 * math.exp(-0.3 * l)
        if l > 0:
            s = _ffn(s, mod, l, 0, 0, n_pre, n_post, wg, wu, wd)
        proj = _mix_in(s, mod, l, n_pre, w_in_b, tables, v_gain, w_s_b, b_s, seg)
        attn = functools.partial(_mix_attn, s, *proj, mod, l, n_post, subln, diff_lambda, sink, w_out_b,
                                 lam_init=lam_init)
        latent = attn(ctx_mode=False)
        if l == DEPTH - 1:
            return _ffn(latent, mod, l, 2, 1, n_pre, n_post, wg, wu, wd)
        s = _ffn(latent, mod, l, 2, 1, n_pre, n_post, wg, wu, wd, ctx=attn(ctx_mode=True))
```

```python
import functools
import math

import jax
import jax.numpy as jnp
from jax import lax
from jax.experimental import pallas as pl
from jax.experimental.pallas import tpu as pltpu

D_MODEL = 1024
BATCH = 8
SEQ = 2048
DEPTH = 4
GRID_W = 64
CTX_LEN = 256
N_MOD = 9
D_FF = int(math.ceil(8 * D_MODEL / 3 / 128)) * 128
FFN_RES = 0.5
ROPE_BASE = 10000.0
EPS = 1e-6
NEG_INF = -1e30

A_WIDTH = D_MODEL // 4
A_HEADS = 4
A_HEAD_DIM = A_WIDTH // A_HEADS
CHUNK = 128
B_WIDTH = D_MODEL // 2
B_HEADS = 4
B_V_DIM = B_WIDTH // B_HEADS
B_QK_DIM = B_V_DIM // 2
C_WIDTH = D_MODEL // 4
C_HEAD_DIM = 64
C_HEADS = C_WIDTH // C_HEAD_DIM
C_KV_HEADS = 2
C_GROUP = C_HEADS // C_KV_HEADS
WINDOW = 128
D_MIX = A_WIDTH + B_WIDTH + C_WIDTH

OFF_UV = 0
OFF_QB = OFF_UV + 2 * A_WIDTH
OFF_QC = OFF_QB + B_HEADS * 2 * B_QK_DIM
OFF_KB = OFF_QC + C_HEADS * C_HEAD_DIM
OFF_VB = OFF_KB + B_HEADS * 2 * B_QK_DIM
OFF_KC = OFF_VB + B_HEADS * B_V_DIM
OFF_VC = OFF_KC + C_KV_HEADS * C_HEAD_DIM
IN_COLS = OFF_VC + C_KV_HEADS * C_HEAD_DIM

T_ALL = SEQ + CTX_LEN
LANES = 128
TM = 256
N_TILES = T_ALL // TM
LATENT_TILES = SEQ // TM
CTX_TILE = N_TILES - 1
MOD_ROWS = 16
CTX_ROW = BATCH
MOD_TN = 1152
SUB_SLAB = 3
SUB_LATENT = 2
BAND = 2 * WINDOW + TM
N_BAND_OFFSETS = 3
LOG2E = math.log2(math.e)
ROW_BLK = 16
VMEM_LIMIT = 56 * 1024 * 1024

F32 = jnp.float32
BF16 = jnp.bfloat16


def _cparams(n_axes):
    return pltpu.CompilerParams(dimension_semantics=("arbitrary",) * n_axes, vmem_limit_bytes=VMEM_LIMIT)


def _rms(xf, g):
    ms = jnp.mean(xf * xf, axis=-1, keepdims=True)
    return xf * lax.rsqrt(ms + EPS) * g


def _dot(a, b):
    return jnp.dot(a, b, preferred_element_type=F32)


def _dot_nt(a, b):
    return lax.dot_general(a, b, (((1,), (1,)), ((), ())), preferred_element_type=F32)


def _resident(block, index):
    return pl.BlockSpec(block, lambda b, t: index, pipeline_mode=pl.Buffered(1))


def _mod_specs(layer):
    batch_row = pl.BlockSpec((None, None, N_MOD, D_MODEL), lambda b, t: (layer, b, 0, 0))
    ctx_row = pl.BlockSpec((None, None, N_MOD, D_MODEL), lambda b, t: (layer, CTX_ROW, 0, 0))
    return [batch_row, ctx_row]


def _mod_kernel(c_ref, w_ref, b_ref, o_ref):
    c = c_ref[...]
    sc = c * jax.nn.sigmoid(c)
    o_ref[...] = _dot(sc.astype(BF16), w_ref[...].astype(BF16)) + b_ref[...]


def _modulation(c_all, w_mod, b_mod):
    n_cols = N_MOD * D_MODEL
    return pl.pallas_call(
        _mod_kernel,
        grid=(DEPTH, n_cols // MOD_TN),
        in_specs=[
            pl.BlockSpec((MOD_ROWS, D_MODEL), lambda l, n: (0, 0)),
            pl.BlockSpec((None, D_MODEL, MOD_TN), lambda l, n: (l, 0, n)),
            pl.BlockSpec((None, 1, MOD_TN), lambda l, n: (l, 0, n)),
        ],
        out_specs=pl.BlockSpec((None, MOD_ROWS, MOD_TN), lambda l, n: (l, 0, n)),
        out_shape=jax.ShapeDtypeStruct((DEPTH, MOD_ROWS, n_cols), F32),
        compiler_params=_cparams(2),
        name="modulation",
    )(c_all, w_mod, b_mod.reshape(DEPTH, 1, n_cols))


def _ffn_kernel(*refs, j, n_sub, has_ctx, split_in):
    if split_in:
        x_refs, ctx_ref, rest = refs[:n_sub], refs[n_sub], refs[n_sub + 1:]
    else:
        s_ref, rest = refs[0], refs[1:]
    modb_ref, modc_ref, gpre_ref, gpost_ref, wg_ref, wu_ref, wd_ref, o_ref = rest
    last = pl.program_id(1) == pl.num_programs(1) - 1
    for i in range(n_sub):
        rows = slice(i * TM, (i + 1) * TM)
        ctx_rows = has_ctx and i == n_sub - 1
        if split_in:
            xf = x_refs[i][...]
            if ctx_rows:
                xf = jnp.where(last, ctx_ref[...], xf)
        else:
            xf = s_ref[rows, :]

        def mod_row(k):
            row = modb_ref[k:k + 1, :]
            return jnp.where(last, modc_ref[k:k + 1, :], row) if ctx_rows else row

        shift, scale, gate = mod_row(3 * j), mod_row(3 * j + 1), mod_row(3 * j + 2)
        y = (_rms(xf, gpre_ref[...] * (1.0 + scale)) + shift).astype(BF16)
        g = _dot(y, wg_ref[...])
        u = _dot(y, wu_ref[...])
        h = (g * jax.nn.sigmoid(g) * u).astype(BF16)
        o = _dot(h, wd_ref[...])
        o_ref[rows, :] = xf + _rms(o, gpost_ref[...] * (FFN_RES * gate))


def _ffn(s, mod, layer, j, jj, norm_pre, norm_post, wg, wu, wd, ctx=None):
    split_in = ctx is not None
    rows = T_ALL if split_in else s.shape[1]
    has_ctx = rows == T_ALL
    n_sub = SUB_SLAB if has_ctx else SUB_LATENT
    tmf = n_sub * TM
    tile = pl.BlockSpec((None, tmf, D_MODEL), lambda b, t: (b, t, 0))
    if split_in:
        def latent_tile(i):
            return pl.BlockSpec((None, TM, D_MODEL),
                                lambda b, t: (b, jnp.minimum(n_sub * t + i, LATENT_TILES - 1), 0))
        data_specs = [latent_tile(i) for i in range(n_sub)]
        data_specs.append(pl.BlockSpec((None, TM, D_MODEL), lambda b, t: (b, 0, 0)))
        data = [s] * n_sub + [ctx]
    else:
        data_specs, data = [tile], [s]
    return pl.pallas_call(
        functools.partial(_ffn_kernel, j=j, n_sub=n_sub, has_ctx=has_ctx, split_in=split_in),
        grid=(BATCH, rows // tmf),
        in_specs=data_specs + _mod_specs(layer) + [
            _resident((None, None, 1, D_MODEL), (layer, j, 0, 0)),
            _resident((None, None, 1, D_MODEL), (layer, j, 0, 0)),
            _resident((None, None, D_MODEL, D_FF), (layer, jj, 0, 0)),
            _resident((None, None, D_MODEL, D_FF), (layer, jj, 0, 0)),
            _resident((None, None, D_FF, D_MODEL), (layer, jj, 0, 0)),
        ],
        out_specs=tile,
        out_shape=jax.ShapeDtypeStruct((BATCH, rows, D_MODEL), F32),
        compiler_params=_cparams(2),
        name="ffn",
    )(*data, mod, mod, norm_pre, norm_post, wg, wu, wd)


def _mix_in_kernel(s_ref, modb_ref, modc_ref, gpre_ref, win_ref, cos_ref, sina_ref, sinb_ref, vgain_ref, ws_ref,
                   bs_ref, seg_ref, oa_ref, qb_ref, qc_ref, kb_ref, vb_ref, kc_ref, vc_ref, *, n_sub):
    last = pl.program_id(1) == pl.num_programs(1) - 1
    quarter = C_HEAD_DIM // 4
    lane = lax.broadcasted_iota(jnp.int32, (TM, LANES), 1)
    low = lane < (LANES // 2)
    a_lane = lax.broadcasted_iota(jnp.int32, (CHUNK, A_WIDTH), 1)
    q_scale = B_QK_DIM ** -0.5 * LOG2E
    c_scale = C_HEAD_DIM ** -0.5 * LOG2E
    ones_blk = jnp.ones((TM, LANES), BF16)

    for i in range(n_sub):
        rows = slice(i * TM, (i + 1) * TM)
        ctx_rows = i == n_sub - 1

        def mod_row(k):
            row = modb_ref[k:k + 1, :]
            return jnp.where(last, modc_ref[k:k + 1, :], row) if ctx_rows else row

        ax = (_rms(s_ref[rows, :], gpre_ref[...] * (1.0 + mod_row(4))) + mod_row(3)).astype(BF16)
        proj = _dot(ax, win_ref[...])

        cos = cos_ref[rows, :]
        sin_a = sina_ref[rows, :]
        sin_b = sinb_ref[rows, :]

        def rope(xb):
            return xb * cos + pltpu.roll(xb, LANES - quarter, 1) * sin_a + pltpu.roll(xb, quarter, 1) * sin_b

        def block(off, n):
            return proj[:, off + n * LANES:off + (n + 1) * LANES]

        for h in range(B_HEADS):
            q = rope(block(OFF_QB, h)) * q_scale
            qb_ref[rows, (2 * h) * LANES:(2 * h + 1) * LANES] = jnp.where(low, q, 0.0).astype(BF16)
            qb_ref[rows, (2 * h + 1) * LANES:(2 * h + 2) * LANES] = jnp.where(low, 0.0, q).astype(BF16)
            kb_ref[rows, h * LANES:(h + 1) * LANES] = rope(block(OFF_KB, h)).astype(BF16)
            vb_ref[rows, (2 * h) * LANES:(2 * h + 1) * LANES] = block(OFF_VB, h).astype(BF16)
            vb_ref[rows, (2 * h + 1) * LANES:(2 * h + 2) * LANES] = ones_blk

        for kv in range(C_KV_HEADS):
            q = rope(block(OFF_QC, kv)) * c_scale
            q_sw = pltpu.roll(q, LANES // 2, 1)
            own_low = kv == 0
            g0 = jnp.where(low, q, 0.0) if own_low else jnp.where(low, 0.0, q_sw)
            g1 = jnp.where(low, q_sw, 0.0) if own_low else jnp.where(low, 0.0, q)
            qc_ref[rows, (2 * kv) * LANES:(2 * kv + 1) * LANES] = g0.astype(BF16)
            qc_ref[rows, (2 * kv + 1) * LANES:(2 * kv + 2) * LANES] = g1.astype(BF16)
        kc_ref[rows, :] = rope(block(OFF_KC, 0)).astype(BF16)
        vc_ref[rows, :] = proj[:, OFF_VC:OFF_VC + LANES].astype(BF16)

        uv = jax.nn.gelu(proj[:, OFF_UV:OFF_UV + 2 * A_WIDTH], approximate=True)
        u = uv[:, :A_WIDTH]
        v = uv[:, A_WIDTH:]
        v2 = v * v
        v2_hi = v2.astype(BF16)
        v2_lo = (v2 - v2_hi.astype(F32)).astype(BF16)
        seg = seg_ref[...]
        ms = (_dot(v2_hi, seg) + _dot(v2_lo, seg)) * (1.0 / A_HEAD_DIM)
        vn = (v * lax.rsqrt(ms + EPS) * vgain_ref[...]).astype(BF16)
        for c in range(TM // CHUNK):
            vc = vn[c * CHUNK:(c + 1) * CHUNK, :]
            mixed = _dot(ws_ref[A_HEADS - 1], vc)
            for h in range(A_HEADS - 2, -1, -1):
                mixed = jnp.where(a_lane < (h + 1) * A_HEAD_DIM, _dot(ws_ref[h], vc), mixed)
            mixed = mixed + bs_ref[...]
            out_rows = slice(i * TM + c * CHUNK, i * TM + (c + 1) * CHUNK)
            oa_ref[out_rows, :] = (u[c * CHUNK:(c + 1) * CHUNK, :] * mixed).astype(BF16)


def _mix_in(s, mod, layer, norm_pre, w_in, tables, v_gain, w_s, b_s, seg):
    cos_t, sin_a, sin_b = tables
    n_sub = SUB_SLAB
    tmi = n_sub * TM
    tok = lambda width: pl.BlockSpec((None, tmi, width), lambda b, t: (b, t, 0))
    pos = pl.BlockSpec((tmi, LANES), lambda b, t: (t, 0))
    out_widths = (A_WIDTH, 2 * B_WIDTH, 2 * C_WIDTH, B_WIDTH, 2 * B_WIDTH, LANES, LANES)
    return pl.pallas_call(
        functools.partial(_mix_in_kernel, n_sub=n_sub),
        grid=(BATCH, T_ALL // tmi),
        in_specs=[tok(D_MODEL)] + _mod_specs(layer) + [
            _resident((None, None, 1, D_MODEL), (layer, 1, 0, 0)),
            _resident((None, D_MODEL, IN_COLS), (layer, 0, 0)),
            pos, pos, pos,
            _resident((None, 1, A_WIDTH), (layer, 0, 0)),
            _resident((None, A_HEADS, CHUNK, CHUNK), (layer, 0, 0, 0)),
            _resident((None, CHUNK, A_WIDTH), (layer, 0, 0)),
            _resident((A_WIDTH, A_WIDTH), (0, 0)),
        ],
        out_specs=[tok(w) for w in out_widths],
        out_shape=[jax.ShapeDtypeStruct((BATCH, T_ALL, w), BF16) for w in out_widths],
        compiler_params=_cparams(2),
        name="mix_in",
    )(s, mod, mod, norm_pre, w_in, cos_t, sin_a, sin_b, v_gain, w_s, b_s, seg)


def _mix_attn_kernel(s_ref, oa_ref, qb_ref, qc_ref, kb_ref, vb_ref, kc_ref, vc_ref, mod_ref, gpost_ref,
                     subln_ref, lamp_ref, sink_ref, wout_ref, band_ref, o_ref, *scratch, n_sub, ctx_mode, lam_init):
    n_keys = CTX_LEN if ctx_mode else T_ALL
    s_scrs, p_scrs = scratch[:n_sub], scratch[n_sub:]
    t = pl.program_id(1)

    lam_p = lamp_ref[...]
    lam = (jnp.exp(jnp.sum(lam_p[0:1, :] * lam_p[1:2, :], axis=-1, keepdims=True))
           - jnp.exp(jnp.sum(lam_p[2:3, :] * lam_p[3:4, :], axis=-1, keepdims=True)) + lam_init)

    row2 = lax.broadcasted_iota(jnp.int32, (2 * TM, 1), 0)
    lane = lax.broadcasted_iota(jnp.int32, (TM, LANES), 1)
    low = lane < (LANES // 2)
    pair = 2 * LANES

    def tile_rows(i):
        return slice(i * TM, (i + 1) * TM)

    def swa_heads(i):
        rows = tile_rows(i)
        if ctx_mode:
            k_all, v_all = kc_ref[...], vc_ref[...]
        else:
            q_pos0 = (t * n_sub + i) * TM
            band_start = pl.multiple_of(jnp.clip(q_pos0 - WINDOW, 0, SEQ - BAND), LANES)
            k_all = jnp.concatenate([kc_ref[pl.ds(band_start, BAND), :], kc_ref[SEQ:T_ALL, :]], axis=0)
            v_all = jnp.concatenate([vc_ref[pl.ds(band_start, BAND), :], vc_ref[SEQ:T_ALL, :]], axis=0)
            bias = band_ref[(q_pos0 - band_start) // WINDOW]
        v_all = jnp.concatenate([v_all, jnp.ones_like(v_all)], axis=1)
        outs = []
        for kv in range(C_KV_HEADS):
            qq = jnp.concatenate([qc_ref[rows, (2 * kv) * LANES:(2 * kv + 1) * LANES],
                                  qc_ref[rows, (2 * kv + 1) * LANES:(2 * kv + 2) * LANES]], axis=0)
            sink = jnp.where(row2 < TM, sink_ref[0:1, 2 * kv:2 * kv + 1],
                             sink_ref[0:1, 2 * kv + 1:2 * kv + 2]) * LOG2E
            s = _dot_nt(qq, k_all)
            if not ctx_mode:
                s = jnp.concatenate([
                    jnp.concatenate([s[:TM, :BAND] + bias, s[:TM, BAND:]], axis=1),
                    jnp.concatenate([s[TM:, :BAND] + bias, s[TM:, BAND:]], axis=1)], axis=0)
            m = jnp.maximum(jnp.max(s, axis=-1, keepdims=True), sink)
            pv = _dot(jnp.exp2(s - m).astype(BF16), v_all)
            o = pv[:, :LANES] * (1.0 / (pv[:, LANES:] + jnp.exp2(sink - m)))
            o_g0, o_g1 = o[:TM], o[TM:]
            if kv == 0:
                blk = jnp.where(low, o_g0, pltpu.roll(o_g1, LANES // 2, 1))
            else:
                blk = jnp.where(low, pltpu.roll(o_g0, LANES // 2, 1), o_g1)
            outs.append(blk.astype(BF16))
        return outs

    def scores(i, h):
        rows = tile_rows(i)
        qq = jnp.concatenate([qb_ref[rows, (2 * h) * LANES:(2 * h + 1) * LANES],
                              qb_ref[rows, (2 * h + 1) * LANES:(2 * h + 2) * LANES]], axis=0)
        s_scrs[i][:, 0:n_keys] = _dot_nt(qq, kb_ref[:, h * LANES:(h + 1) * LANES])

    def diff_head(i, h):
        s_scr, p_scr = s_scrs[i], p_scrs[2 * i + h % 2]
        m = jnp.max(s_scr[:, 0:n_keys], axis=-1, keepdims=True)
        for rb in range(2 * TM // ROW_BLK):
            rows = slice(rb * ROW_BLK, (rb + 1) * ROW_BLK)
            m_blk = m[rows]
            for c in range(n_keys // LANES):
                cols = slice(c * LANES, (c + 1) * LANES)
                p_scr[rows, cols] = jnp.exp2(s_scr[rows, cols] - m_blk).astype(BF16)
        pv = _dot(p_scr[:, 0:n_keys], vb_ref[:, (2 * h) * LANES:(2 * h + 2) * LANES])
        num, r = pv[:, 0:LANES], 1.0 / pv[:, LANES:2 * LANES]
        o = num[:TM] * r[:TM] - num[TM:] * (lam * r[TM:])
        return _rms(o, subln_ref[...] * (1.0 - lam_init)).astype(BF16)

    mix = []
    for i in range(n_sub):
        oc = swa_heads(i)
        part = _dot(oa_ref[tile_rows(i), :], wout_ref[0:A_WIDTH, :])
        mix.append(part + _dot(jnp.concatenate(oc, axis=-1), wout_ref[A_WIDTH + B_WIDTH:D_MIX, :]))
    for i in range(n_sub):
        scores(i, 0)
    ob = [[] for _ in range(n_sub)]
    for h in range(B_HEADS):
        for i in range(n_sub):
            ob[i].append(diff_head(i, h))
            if h + 1 < B_HEADS:
                scores(i, h + 1)
            if h % 2 == 1:
                w_rows = slice(A_WIDTH + (h // 2) * pair, A_WIDTH + (h // 2 + 1) * pair)
                mix[i] = mix[i] + _dot(jnp.concatenate(ob[i][h - 1:h + 1], axis=-1), wout_ref[w_rows, :])
    gate = mod_ref[5:6, :]
    for i in range(n_sub):
        rows = tile_rows(i)
        o_ref[rows, :] = s_ref[rows, :] + _rms(mix[i], gpost_ref[...] * gate)


def _mix_attn(s, oa, qb, qc, kb, vb, kc, vc, mod, layer, norm_post, subln, lam_p, sink, w_out, band_bias, *,
              ctx_mode, lam_init):
    n_sub = 1 if ctx_mode else SUB_LATENT
    tma = n_sub * TM
    out_rows = CTX_LEN if ctx_mode else SEQ
    n_keys = CTX_LEN if ctx_mode else T_ALL
    t0 = CTX_TILE if ctx_mode else 0
    tok = lambda width: pl.BlockSpec((None, tma, width), lambda b, t: (b, t + t0, 0))
    if ctx_mode:
        keys = lambda width: pl.BlockSpec((None, CTX_LEN, width), lambda b, t: (b, CTX_TILE, 0))
        mod_spec = pl.BlockSpec((None, None, N_MOD, D_MODEL), lambda b, t: (layer, CTX_ROW, 0, 0))
    else:
        keys = lambda width: pl.BlockSpec((None, T_ALL, width), lambda b, t: (b, 0, 0))
        mod_spec = pl.BlockSpec((None, None, N_MOD, D_MODEL), lambda b, t: (layer, b, 0, 0))
    return pl.pallas_call(
        functools.partial(_mix_attn_kernel, n_sub=n_sub, ctx_mode=ctx_mode, lam_init=lam_init),
        grid=(BATCH, out_rows // tma),
        in_specs=[
            tok(D_MODEL), tok(A_WIDTH), tok(2 * B_WIDTH), tok(2 * C_WIDTH),
            keys(B_WIDTH), keys(2 * B_WIDTH), keys(LANES), keys(LANES),
            mod_spec,
            _resident((None, None, 1, D_MODEL), (layer, 1, 0, 0)),
            _resident((None, 1, B_V_DIM), (layer, 0, 0)),
            _resident((None, 4, B_QK_DIM), (layer, 0, 0)),
            _resident((None, 1, C_HEADS), (layer, 0, 0)),
            _resident((None, D_MIX, D_MODEL), (layer, 0, 0)),
            _resident((N_BAND_OFFSETS, TM, BAND), (0, 0, 0)),
        ],
        out_specs=pl.BlockSpec((None, tma, D_MODEL), lambda b, t: (b, t, 0)),
        out_shape=jax.ShapeDtypeStruct((BATCH, out_rows, D_MODEL), F32),
        scratch_shapes=([pltpu.VMEM((2 * TM, n_keys), F32)] * n_sub
                        + [pltpu.VMEM((2 * TM, n_keys), BF16)] * (2 * n_sub)),
        compiler_params=_cparams(2),
        name="mix_attn_ctx" if ctx_mode else "mix_attn",
    )(s, oa, qb, qc, kb, vb, kc, vc, mod, norm_post, subln, lam_p, sink, w_out, band_bias)


def _rope_tables():
    rows = SEQ // GRID_W
    row = jnp.repeat(jnp.arange(rows, dtype=F32), GRID_W)
    col = jnp.tile(jnp.arange(GRID_W, dtype=F32), rows)
    quarter = C_HEAD_DIM // 4
    inv = ROPE_BASE ** (-jnp.arange(quarter, dtype=F32) / quarter)
    ar = row[:, None] * inv[None, :]
    ac = col[:, None] * inv[None, :]
    ang = jnp.concatenate([ar, ar, ac, ac] * (LANES // C_HEAD_DIM), axis=-1)
    cos = jnp.concatenate([jnp.cos(ang), jnp.ones((CTX_LEN, LANES), F32)], axis=0)
    sin = jnp.concatenate([jnp.sin(ang), jnp.zeros((CTX_LEN, LANES), F32)], axis=0)
    first = (jnp.arange(LANES) // quarter) % 2 == 0
    sin_a = jnp.where(first[None, :], -sin, 0.0)
    sin_b = jnp.where(first[None, :], 0.0, sin)
    return cos, sin_a, sin_b


def kernel(x, c, ctx, c_ctx, w_mod, b_mod, norm_pre, norm_post, ffn_w_gate, ffn_w_up, ffn_w_down, w_in, w_out,
           gmlp_v_gain, gmlp_w_s, gmlp_b_s, diff_lambda, diff_subln, swa_sink):
    assert x.shape == (BATCH, SEQ, D_MODEL) and ctx.shape == (BATCH, CTX_LEN, D_MODEL)
    c_all = jnp.zeros((MOD_ROWS, D_MODEL), F32).at[:BATCH].set(c).at[CTX_ROW].set(c_ctx)
    mod = _modulation(c_all, w_mod, b_mod).reshape(DEPTH, MOD_ROWS, N_MOD, D_MODEL)

    tables = _rope_tables()
    q_in_band = jnp.arange(TM)[None, :, None] + WINDOW * jnp.arange(N_BAND_OFFSETS)[:, None, None]
    band_bias = jnp.where(jnp.abs(q_in_band - jnp.arange(BAND)[None, None, :]) <= WINDOW, 0.0, NEG_INF).astype(F32)
    head_of = jnp.arange(A_WIDTH) // A_HEAD_DIM
    seg = (head_of[:, None] == head_of[None, :]).astype(BF16)
    wg, wu, wd = ffn_w_gate.astype(BF16), ffn_w_up.astype(BF16), ffn_w_down.astype(BF16)
    w_in_b, w_out_b, w_s_b = w_in.astype(BF16), w_out.astype(BF16), gmlp_w_s.astype(BF16)
    b_s = jnp.repeat(jnp.swapaxes(gmlp_b_s, 1, 2), A_HEAD_DIM, axis=2)
    n_pre = norm_pre.reshape(DEPTH, 3, 1, D_MODEL)
    n_post = norm_post.reshape(DEPTH, 3, 1, D_MODEL)
    v_gain = gmlp_v_gain.reshape(DEPTH, 1, A_WIDTH)
    subln = diff_subln.reshape(DEPTH, 1, B_V_DIM)
    sink = swa_sink.reshape(DEPTH, 1, C_HEADS)

    s = _ffn(x, mod, 0, 0, 0, n_pre, n_post, wg, wu, wd, ctx=ctx)
    for l in range(DEPTH):
        lam_init = 0.8 - 0.6 * math.exp(-0.3 * l)
        if l > 0:
            s = _ffn(s, mod, l, 0, 0, n_pre, n_post, wg, wu, wd)
        proj = _mix_in(s, mod, l, n_pre, w_in_b, tables, v_gain, w_s_b, b_s, seg)
        attn = functools.partial(_mix_attn, s, *proj, mod, l, n_post, subln, diff_lambda, sink, w_out_b,
                                 band_bias, lam_init=lam_init)
        latent = attn(ctx_mode=False)
        if l == DEPTH - 1:
            return _ffn(latent, mod, l, 2, 1, n_pre, n_post, wg, wu, wd)
        s = _ffn(latent, mod, l, 2, 1, n_pre, n_post, wg, wu, wd, ctx=attn(ctx_mode=True))
```

```python
import functools
import math

import jax
import jax.numpy as jnp
from jax import lax
from jax.experimental import pallas as pl
from jax.experimental.pallas import tpu as pltpu

D_MODEL = 1024
BATCH = 8
SEQ = 2048
DEPTH = 4
GRID_W = 64
CTX_LEN = 256
N_MOD = 9
D_FF = int(math.ceil(8 * D_MODEL / 3 / 128)) * 128
FFN_RES = 0.5
ROPE_BASE = 10000.0
EPS = 1e-6
NEG_INF = -1e30

A_WIDTH = D_MODEL // 4
A_HEADS = 4
A_HEAD_DIM = A_WIDTH // A_HEADS
CHUNK = 128
B_WIDTH = D_MODEL // 2
B_HEADS = 4
B_V_DIM = B_WIDTH // B_HEADS
B_QK_DIM = B_V_DIM // 2
C_WIDTH = D_MODEL // 4
C_HEAD_DIM = 64
C_HEADS = C_WIDTH // C_HEAD_DIM
C_KV_HEADS = 2
C_GROUP = C_HEADS // C_KV_HEADS
WINDOW = 128
D_MIX = A_WIDTH + B_WIDTH + C_WIDTH

OFF_UV = 0
OFF_QB = OFF_UV + 2 * A_WIDTH
OFF_QC = OFF_QB + B_HEADS * 2 * B_QK_DIM
OFF_KB = OFF_QC + C_HEADS * C_HEAD_DIM
OFF_VB = OFF_KB + B_HEADS * 2 * B_QK_DIM
OFF_KC = OFF_VB + B_HEADS * B_V_DIM
OFF_VC = OFF_KC + C_KV_HEADS * C_HEAD_DIM
IN_COLS = OFF_VC + C_KV_HEADS * C_HEAD_DIM

T_ALL = SEQ + CTX_LEN
LANES = 128
TM = 256
N_TILES = T_ALL // TM
LATENT_TILES = SEQ // TM
CTX_TILE = N_TILES - 1
MOD_ROWS = 16
CTX_ROW = BATCH
MOD_TN = 1152
SUB_SLAB = 3
SUB_LATENT = 2
BAND = 2 * WINDOW + TM
N_BAND_OFFSETS = 3
LOG2E = math.log2(math.e)
ROW_BLK = 16
VMEM_LIMIT = 56 * 1024 * 1024

F32 = jnp.float32
BF16 = jnp.bfloat16


def _cparams(n_axes):
    return pltpu.CompilerParams(dimension_semantics=("arbitrary",) * n_axes, vmem_limit_bytes=VMEM_LIMIT)


def _rms(xf, g):
    ms = jnp.mean(xf * xf, axis=-1, keepdims=True)
    return xf * lax.rsqrt(ms + EPS) * g


def _dot(a, b):
    return jnp.dot(a, b, preferred_element_type=F32)


def _dot_nt(a, b):
    return lax.dot_general(a, b, (((1,), (1,)), ((), ())), preferred_element_type=F32)


def _resident(block, index):
    return pl.BlockSpec(block, lambda b, t: index, pipeline_mode=pl.Buffered(1))


def _mod_specs(layer):
    batch_row = pl.BlockSpec((None, None, N_MOD, D_MODEL), lambda b, t: (layer, b, 0, 0))
    ctx_row = pl.BlockSpec((None, None, N_MOD, D_MODEL), lambda b, t: (layer, CTX_ROW, 0, 0))
    return [batch_row, ctx_row]


def _mod_kernel(c_ref, w_ref, b_ref, o_ref):
    c = c_ref[...]
    sc = c * jax.nn.sigmoid(c)
    o_ref[...] = _dot(sc.astype(BF16), w_ref[...].astype(BF16)) + b_ref[...]


def _modulation(c_all, w_mod, b_mod):
    n_cols = N_MOD * D_MODEL
    return pl.pallas_call(
        _mod_kernel,
        grid=(DEPTH, n_cols // MOD_TN),
        in_specs=[
            pl.BlockSpec((MOD_ROWS, D_MODEL), lambda l, n: (0, 0)),
            pl.BlockSpec((None, D_MODEL, MOD_TN), lambda l, n: (l, 0, n)),
            pl.BlockSpec((None, 1, MOD_TN), lambda l, n: (l, 0, n)),
        ],
        out_specs=pl.BlockSpec((None, MOD_ROWS, MOD_TN), lambda l, n: (l, 0, n)),
        out_shape=jax.ShapeDtypeStruct((DEPTH, MOD_ROWS, n_cols), F32),
        compiler_params=_cparams(2),
        name="modulation",
    )(c_all, w_mod, b_mod.reshape(DEPTH, 1, n_cols))


def _ffn_kernel(*refs, j, n_sub, has_ctx, split_in):
    if split_in:
        x_refs, ctx_ref, rest = refs[:n_sub], refs[n_sub], refs[n_sub + 1:]
    else:
        s_ref, rest = refs[0], refs[1:]
    modb_ref, modc_ref, gpre_ref, gpost_ref, wg_ref, wu_ref, wd_ref, o_ref = rest
    last = pl.program_id(1) == pl.num_programs(1) - 1
    for i in range(n_sub):
        rows = slice(i * TM, (i + 1) * TM)
        ctx_rows = has_ctx and i == n_sub - 1
        if split_in:
            xf = x_refs[i][...]
            if ctx_rows:
                xf = jnp.where(last, ctx_ref[...], xf)
        else:
            xf = s_ref[rows, :]

        def mod_row(k):
            row = modb_ref[k:k + 1, :]
            return jnp.where(last, modc_ref[k:k + 1, :], row) if ctx_rows else row

        shift, scale, gate = mod_row(3 * j), mod_row(3 * j + 1), mod_row(3 * j + 2)
        y = (_rms(xf, gpre_ref[...] * (1.0 + scale)) + shift).astype(BF16)
        g = _dot(y, wg_ref[...])
        u = _dot(y, wu_ref[...])
        h = (g * jax.nn.sigmoid(g) * u).astype(BF16)
        o = _dot(h, wd_ref[...])
        o_ref[rows, :] = xf + _rms(o, gpost_ref[...] * (FFN_RES * gate))


def _ffn(s, mod, layer, j, jj, norm_pre, norm_post, wg, wu, wd, ctx=None):
    split_in = ctx is not None
    rows = T_ALL if split_in else s.shape[1]
    has_ctx = rows == T_ALL
    n_sub = SUB_SLAB if has_ctx else SUB_LATENT
    tmf = n_sub * TM
    tile = pl.BlockSpec((None, tmf, D_MODEL), lambda b, t: (b, t, 0))
    if split_in:
        def latent_tile(i):
            return pl.BlockSpec((None, TM, D_MODEL),
                                lambda b, t: (b, jnp.minimum(n_sub * t + i, LATENT_TILES - 1), 0))
        data_specs = [latent_tile(i) for i in range(n_sub)]
        data_specs.append(pl.BlockSpec((None, TM, D_MODEL), lambda b, t: (b, 0, 0)))
        data = [s] * n_sub + [ctx]
    else:
        data_specs, data = [tile], [s]
    return pl.pallas_call(
        functools.partial(_ffn_kernel, j=j, n_sub=n_sub, has_ctx=has_ctx, split_in=split_in),
        grid=(BATCH, rows // tmf),
        in_specs=data_specs + _mod_specs(layer) + [
            _resident((None, None, 1, D_MODEL), (layer, j, 0, 0)),
            _resident((None, None, 1, D_MODEL), (layer, j, 0, 0)),
            _resident((None, None, D_MODEL, D_FF), (layer, jj, 0, 0)),
            _resident((None, None, D_MODEL, D_FF), (layer, jj, 0, 0)),
            _resident((None, None, D_FF, D_MODEL), (layer, jj, 0, 0)),
        ],
        out_specs=tile,
        out_shape=jax.ShapeDtypeStruct((BATCH, rows, D_MODEL), F32),
        compiler_params=_cparams(2),
        name="ffn",
    )(*data, mod, mod, norm_pre, norm_post, wg, wu, wd)


def _mix_in_kernel(s_ref, modb_ref, modc_ref, gpre_ref, win_ref, cos_ref, sina_ref, sinb_ref, vgain_ref, ws_ref,
                   bs_ref, seg_ref, oa_ref, qb_ref, qc_ref, kb_ref, vb_ref, kc_ref, vc_ref, *, n_sub):
    last = pl.program_id(1) == pl.num_programs(1) - 1
    quarter = C_HEAD_DIM // 4
    lane = lax.broadcasted_iota(jnp.int32, (TM, LANES), 1)
    low = lane < (LANES // 2)
    a_lane = lax.broadcasted_iota(jnp.int32, (CHUNK, A_WIDTH), 1)
    q_scale = B_QK_DIM ** -0.5 * LOG2E
    c_scale = C_HEAD_DIM ** -0.5 * LOG2E
    ones_blk = jnp.ones((TM, LANES), BF16)

    for i in range(n_sub):
        rows = slice(i * TM, (i + 1) * TM)
        ctx_rows = i == n_sub - 1

        def mod_row(k):
            row = modb_ref[k:k + 1, :]
            return jnp.where(last, modc_ref[k:k + 1, :], row) if ctx_rows else row

        ax = (_rms(s_ref[rows, :], gpre_ref[...] * (1.0 + mod_row(4))) + mod_row(3)).astype(BF16)
        proj = _dot(ax, win_ref[...])

        cos = cos_ref[rows, :]
        sin_a = sina_ref[rows, :]
        sin_b = sinb_ref[rows, :]

        def rope(xb):
            return xb * cos + pltpu.roll(xb, LANES - quarter, 1) * sin_a + pltpu.roll(xb, quarter, 1) * sin_b

        def block(off, n):
            return proj[:, off + n * LANES:off + (n + 1) * LANES]

        for h in range(B_HEADS):
            q = rope(block(OFF_QB, h)) * q_scale
            qb_ref[rows, (2 * h) * LANES:(2 * h + 1) * LANES] = jnp.where(low, q, 0.0).astype(BF16)
            qb_ref[rows, (2 * h + 1) * LANES:(2 * h + 2) * LANES] = jnp.where(low, 0.0, q).astype(BF16)
            kb_ref[rows, h * LANES:(h + 1) * LANES] = rope(block(OFF_KB, h)).astype(BF16)
            vb_ref[rows, (2 * h) * LANES:(2 * h + 1) * LANES] = block(OFF_VB, h).astype(BF16)
            vb_ref[rows, (2 * h + 1) * LANES:(2 * h + 2) * LANES] = ones_blk

        for kv in range(C_KV_HEADS):
            q = rope(block(OFF_QC, kv)) * c_scale
            q_sw = pltpu.roll(q, LANES // 2, 1)
            own_low = kv == 0
            g0 = jnp.where(low, q, 0.0) if own_low else jnp.where(low, 0.0, q_sw)
            g1 = jnp.where(low, q_sw, 0.0) if own_low else jnp.where(low, 0.0, q)
            qc_ref[rows, (2 * kv) * LANES:(2 * kv + 1) * LANES] = g0.astype(BF16)
            qc_ref[rows, (2 * kv + 1) * LANES:(2 * kv + 2) * LANES] = g1.astype(BF16)
        kc_ref[rows, :] = rope(block(OFF_KC, 0)).astype(BF16)
        vc_ref[rows, :] = proj[:, OFF_VC:OFF_VC + LANES].astype(BF16)

        uv = jax.nn.gelu(proj[:, OFF_UV:OFF_UV + 2 * A_WIDTH], approximate=True)
        u = uv[:, :A_WIDTH]
        v = uv[:, A_WIDTH:]
        v2 = v * v
        v2_hi = v2.astype(BF16)
        v2_lo = (v2 - v2_hi.astype(F32)).astype(BF16)
        seg = seg_ref[...]
        ms = (_dot(v2_hi, seg) + _dot(v2_lo, seg)) * (1.0 / A_HEAD_DIM)
        vn = (v * lax.rsqrt(ms + EPS) * vgain_ref[...]).astype(BF16)
        for c in range(TM // CHUNK):
            vc = vn[c * CHUNK:(c + 1) * CHUNK, :]
            mixed = _dot(ws_ref[A_HEADS - 1], vc)
            for h in range(A_HEADS - 2, -1, -1):
                mixed = jnp.where(a_lane < (h + 1) * A_HEAD_DIM, _dot(ws_ref[h], vc), mixed)
            mixed = mixed + bs_ref[...]
            out_rows = slice(i * TM + c * CHUNK, i * TM + (c + 1) * CHUNK)
            oa_ref[out_rows, :] = (u[c * CHUNK:(c + 1) * CHUNK, :] * mixed).astype(BF16)


def _mix_in(s, mod, layer, norm_pre, w_in, tables, v_gain, w_s, b_s, seg):
    cos_t, sin_a, sin_b = tables
    n_sub = SUB_SLAB
    tmi = n_sub * TM
    tok = lambda width: pl.BlockSpec((None, tmi, width), lambda b, t: (b, t, 0))
    pos = pl.BlockSpec((tmi, LANES), lambda b, t: (t, 0))
    out_widths = (A_WIDTH, 2 * B_WIDTH, 2 * C_WIDTH, B_WIDTH, 2 * B_WIDTH, LANES, LANES)
    return pl.pallas_call(
        functools.partial(_mix_in_kernel, n_sub=n_sub),
        grid=(BATCH, T_ALL // tmi),
        in_specs=[tok(D_MODEL)] + _mod_specs(layer) + [
            _resident((None, None, 1, D_MODEL), (layer, 1, 0, 0)),
            _resident((None, D_MODEL, IN_COLS), (layer, 0, 0)),
            pos, pos, pos,
            _resident((None, 1, A_WIDTH), (layer, 0, 0)),
            _resident((None, A_HEADS, CHUNK, CHUNK), (layer, 0, 0, 0)),
            _resident((None, CHUNK, A_WIDTH), (layer, 0, 0)),
            _resident((A_WIDTH, A_WIDTH), (0, 0)),
        ],
        out_specs=[tok(w) for w in out_widths],
        out_shape=[jax.ShapeDtypeStruct((BATCH, T_ALL, w), BF16) for w in out_widths],
        compiler_params=_cparams(2),
        name="mix_in",
    )(s, mod, mod, norm_pre, w_in, cos_t, sin_a, sin_b, v_gain, w_s, b_s, seg)


def _mix_attn_kernel(s_ref, oa_ref, qb_ref, qc_ref, kb_ref, vb_ref, kc_ref, vc_ref, mod_ref, gpost_ref,
                     subln_ref, lamp_ref, sink_ref, wout_ref, band_ref, o_ref, *scratch, n_sub, ctx_mode, lam_init):
    n_keys = CTX_LEN if ctx_mode else T_ALL
    s_scrs, p_scrs = scratch[:n_sub], scratch[n_sub:]
    t = pl.program_id(1)

    lam_p = lamp_ref[...]
    lam = (jnp.exp(jnp.sum(lam_p[0:1, :] * lam_p[1:2, :], axis=-1, keepdims=True))
           - jnp.exp(jnp.sum(lam_p[2:3, :] * lam_p[3:4, :], axis=-1, keepdims=True)) + lam_init)

    row2 = lax.broadcasted_iota(jnp.int32, (2 * TM, 1), 0)
    lane = lax.broadcasted_iota(jnp.int32, (TM, LANES), 1)
    low = lane < (LANES // 2)
    pair = 2 * LANES

    def tile_rows(i):
        return slice(i * TM, (i + 1) * TM)

    def swa_heads(i):
        rows = tile_rows(i)
        if ctx_mode:
            k_all, v_all = kc_ref[...], vc_ref[...]
        else:
            q_pos0 = (t * n_sub + i) * TM
            band_start = pl.multiple_of(jnp.clip(q_pos0 - WINDOW, 0, SEQ - BAND), LANES)
            k_all = jnp.concatenate([kc_ref[pl.ds(band_start, BAND), :], kc_ref[SEQ:T_ALL, :]], axis=0)
            v_all = jnp.concatenate([vc_ref[pl.ds(band_start, BAND), :], vc_ref[SEQ:T_ALL, :]], axis=0)
            bias = band_ref[(q_pos0 - band_start) // WINDOW]
        v_all = jnp.concatenate([v_all, jnp.ones_like(v_all)], axis=1)
        outs = []
        for kv in range(C_KV_HEADS):
            qq = jnp.concatenate([qc_ref[rows, (2 * kv) * LANES:(2 * kv + 1) * LANES],
                                  qc_ref[rows, (2 * kv + 1) * LANES:(2 * kv + 2) * LANES]], axis=0)
            sink = jnp.where(row2 < TM, sink_ref[0:1, 2 * kv:2 * kv + 1],
                             sink_ref[0:1, 2 * kv + 1:2 * kv + 2]) * LOG2E
            s = _dot_nt(qq, k_all)
            if not ctx_mode:
                s = jnp.concatenate([
                    jnp.concatenate([s[:TM, :BAND] + bias, s[:TM, BAND:]], axis=1),
                    jnp.concatenate([s[TM:, :BAND] + bias, s[TM:, BAND:]], axis=1)], axis=0)
            m = jnp.maximum(jnp.max(s, axis=-1, keepdims=True), sink)
            pv = _dot(jnp.exp2(s - m).astype(BF16), v_all)
            o = pv[:, :LANES] * (1.0 / (pv[:, LANES:] + jnp.exp2(sink - m)))
            o_g0, o_g1 = o[:TM], o[TM:]
            if kv == 0:
                blk = jnp.where(low, o_g0, pltpu.roll(o_g1, LANES // 2, 1))
            else:
                blk = jnp.where(low, pltpu.roll(o_g0, LANES // 2, 1), o_g1)
            outs.append(blk.astype(BF16))
        return outs

    row_max = [None] * n_sub

    def scores(i, h):
        rows = tile_rows(i)
        qq = jnp.concatenate([qb_ref[rows, (2 * h) * LANES:(2 * h + 1) * LANES],
                              qb_ref[rows, (2 * h + 1) * LANES:(2 * h + 2) * LANES]], axis=0)
        sv = _dot_nt(qq, kb_ref[:, h * LANES:(h + 1) * LANES])
        s_scrs[i][:, 0:n_keys] = sv
        row_max[i] = jnp.max(sv, axis=-1, keepdims=True)

    def diff_head(i, h):
        s_scr, p_scr = s_scrs[i], p_scrs[2 * i + h % 2]
        m = row_max[i]
        for rb in range(2 * TM // ROW_BLK):
            rows = slice(rb * ROW_BLK, (rb + 1) * ROW_BLK)
            m_blk = m[rows]
            for c in range(n_keys // LANES):
                cols = slice(c * LANES, (c + 1) * LANES)
                p_scr[rows, cols] = jnp.exp2(s_scr[rows, cols] - m_blk).astype(BF16)
        pv = _dot(p_scr[:, 0:n_keys], vb_ref[:, (2 * h) * LANES:(2 * h + 2) * LANES])
        num, r = pv[:, 0:LANES], 1.0 / pv[:, LANES:2 * LANES]
        o = num[:TM] * r[:TM] - num[TM:] * (lam * r[TM:])
        return _rms(o, subln_ref[...] * (1.0 - lam_init)).astype(BF16)

    mix = []
    for i in range(n_sub):
        oc = swa_heads(i)
        part = _dot(oa_ref[tile_rows(i), :], wout_ref[0:A_WIDTH, :])
        mix.append(part + _dot(jnp.concatenate(oc, axis=-1), wout_ref[A_WIDTH + B_WIDTH:D_MIX, :]))
    for i in range(n_sub):
        scores(i, 0)
    ob = [[] for _ in range(n_sub)]
    for h in range(B_HEADS):
        for i in range(n_sub):
            ob[i].append(diff_head(i, h))
            if h + 1 < B_HEADS:
                scores(i, h + 1)
            if h % 2 == 1:
                w_rows = slice(A_WIDTH + (h // 2) * pair, A_WIDTH + (h // 2 + 1) * pair)
                mix[i] = mix[i] + _dot(jnp.concatenate(ob[i][h - 1:h + 1], axis=-1), wout_ref[w_rows, :])
    gate = mod_ref[5:6, :]
    for i in range(n_sub):
        rows = tile_rows(i)
        o_ref[rows, :] = s_ref[rows, :] + _rms(mix[i], gpost_ref[...] * gate)


def _mix_attn(s, oa, qb, qc, kb, vb, kc, vc, mod, layer, norm_post, subln, lam_p, sink, w_out, band_bias, *,
              ctx_mode, lam_init):
    n_sub = 1 if ctx_mode else SUB_LATENT
    tma = n_sub * TM
    out_rows = CTX_LEN if ctx_mode else SEQ
    n_keys = CTX_LEN if ctx_mode else T_ALL
    t0 = CTX_TILE if ctx_mode else 0
    tok = lambda width: pl.BlockSpec((None, tma, width), lambda b, t: (b, t + t0, 0))
    if ctx_mode:
        keys = lambda width: pl.BlockSpec((None, CTX_LEN, width), lambda b, t: (b, CTX_TILE, 0))
        mod_spec = pl.BlockSpec((None, None, N_MOD, D_MODEL), lambda b, t: (layer, CTX_ROW, 0, 0))
    else:
        keys = lambda width: pl.BlockSpec((None, T_ALL, width), lambda b, t: (b, 0, 0))
        mod_spec = pl.BlockSpec((None, None, N_MOD, D_MODEL), lambda b, t: (layer, b, 0, 0))
    return pl.pallas_call(
        functools.partial(_mix_attn_kernel, n_sub=n_sub, ctx_mode=ctx_mode, lam_init=lam_init),
        grid=(BATCH, out_rows // tma),
        in_specs=[
            tok(D_MODEL), tok(A_WIDTH), tok(2 * B_WIDTH), tok(2 * C_WIDTH),
            keys(B_WIDTH), keys(2 * B_WIDTH), keys(LANES), keys(LANES),
            mod_spec,
            _resident((None, None, 1, D_MODEL), (layer, 1, 0, 0)),
            _resident((None, 1, B_V_DIM), (layer, 0, 0)),
            _resident((None, 4, B_QK_DIM), (layer, 0, 0)),
            _resident((None, 1, C_HEADS), (layer, 0, 0)),
            _resident((None, D_MIX, D_MODEL), (layer, 0, 0)),
            _resident((N_BAND_OFFSETS, TM, BAND), (0, 0, 0)),
        ],
        out_specs=pl.BlockSpec((None, tma, D_MODEL), lambda b, t: (b, t, 0)),
        out_shape=jax.ShapeDtypeStruct((BATCH, out_rows, D_MODEL), F32),
        scratch_shapes=([pltpu.VMEM((2 * TM, n_keys), F32)] * n_sub
                        + [pltpu.VMEM((2 * TM, n_keys), BF16)] * (2 * n_sub)),
        compiler_params=_cparams(2),
        name="mix_attn_ctx" if ctx_mode else "mix_attn",
    )(s, oa, qb, qc, kb, vb, kc, vc, mod, norm_post, subln, lam_p, sink, w_out, band_bias)


def _rope_tables():
    rows = SEQ // GRID_W
    row = jnp.repeat(jnp.arange(rows, dtype=F32), GRID_W)
    col = jnp.tile(jnp.arange(GRID_W, dtype=F32), rows)
    quarter = C_HEAD_DIM // 4
    inv = ROPE_BASE ** (-jnp.arange(quarter, dtype=F32) / quarter)
    ar = row[:, None] * inv[None, :]
    ac = col[:, None] * inv[None, :]
    ang = jnp.concatenate([ar, ar, ac, ac] * (LANES // C_HEAD_DIM), axis=-1)
    cos = jnp.concatenate([jnp.cos(ang), jnp.ones((CTX_LEN, LANES), F32)], axis=0)
    sin = jnp.concatenate([jnp.sin(ang), jnp.zeros((CTX_LEN, LANES), F32)], axis=0)
    first = (jnp.arange(LANES) // quarter) % 2 == 0
    sin_a = jnp.where(first[None, :], -sin, 0.0)
    sin_b = jnp.where(first[None, :], 0.0, sin)
    return cos, sin_a, sin_b


def kernel(x, c, ctx, c_ctx, w_mod, b_mod, norm_pre, norm_post, ffn_w_gate, ffn_w_up, ffn_w_down, w_in, w_out,
           gmlp_v_gain, gmlp_w_s, gmlp_b_s, diff_lambda, diff_subln, swa_sink):
    assert x.shape == (BATCH, SEQ, D_MODEL) and ctx.shape == (BATCH, CTX_LEN, D_MODEL)
    c_all = jnp.zeros((MOD_ROWS, D_MODEL), F32).at[:BATCH].set(c).at[CTX_ROW].set(c_ctx)
    mod = _modulation(c_all, w_mod, b_mod).reshape(DEPTH, MOD_ROWS, N_MOD, D_MODEL)

    tables = _rope_tables()
    q_in_band = jnp.arange(TM)[None, :, None] + WINDOW * jnp.arange(N_BAND_OFFSETS)[:, None, None]
    band_bias = jnp.where(jnp.abs(q_in_band - jnp.arange(BAND)[None, None, :]) <= WINDOW, 0.0, NEG_INF).astype(F32)
    head_of = jnp.arange(A_WIDTH) // A_HEAD_DIM
    seg = (head_of[:, None] == head_of[None, :]).astype(BF16)
    wg, wu, wd = ffn_w_gate.astype(BF16), ffn_w_up.astype(BF16), ffn_w_down.astype(BF16)
    w_in_b, w_out_b, w_s_b = w_in.astype(BF16), w_out.astype(BF16), gmlp_w_s.astype(BF16)
    b_s = jnp.repeat(jnp.swapaxes(gmlp_b_s, 1, 2), A_HEAD_DIM, axis=2)
    n_pre = norm_pre.reshape(DEPTH, 3, 1, D_MODEL)
    n_post = norm_post.reshape(DEPTH, 3, 1, D_MODEL)
    v_gain = gmlp_v_gain.reshape(DEPTH, 1, A_WIDTH)
    subln = diff_subln.reshape(DEPTH, 1, B_V_DIM)
    sink = swa_sink.reshape(DEPTH, 1, C_HEADS)

    s = _ffn(x, mod, 0, 0, 0, n_pre, n_post, wg, wu, wd, ctx=ctx)
    for l in range(DEPTH):
        lam_init = 0.8 - 0.6 * math.exp(-0.3 * l)
        if l > 0:
            s = _ffn(s, mod, l, 0, 0, n_pre, n_post, wg, wu, wd)
        proj = _mix_in(s, mod, l, n_pre, w_in_b, tables, v_gain, w_s_b, b_s, seg)
        attn = functools.partial(_mix_attn, s, *proj, mod, l, n_post, subln, diff_lambda, sink, w_out_b,
                                 band_bias, lam_init=lam_init)
        latent = attn(ctx_mode=False)
        if l == DEPTH - 1:
            return _ffn(latent, mod, l, 2, 1, n_pre, n_post, wg, wu, wd)
        s = _ffn(latent, mod, l, 2, 1, n_pre, n_post, wg, wu, wd, ctx=attn(ctx_mode=True))
```

```python
import functools
import math

import jax
import jax.numpy as jnp
from jax import lax
from jax.experimental import pallas as pl
from jax.experimental.pallas import tpu as pltpu

D_MODEL = 1024
BATCH = 8
SEQ = 2048
DEPTH = 4
GRID_W = 64
CTX_LEN = 256
N_MOD = 9
D_FF = int(math.ceil(8 * D_MODEL / 3 / 128)) * 128
FFN_RES = 0.5
ROPE_BASE = 10000.0
EPS = 1e-6
NEG_INF = -1e30

A_WIDTH = D_MODEL // 4
A_HEADS = 4
A_HEAD_DIM = A_WIDTH // A_HEADS
CHUNK = 128
B_WIDTH = D_MODEL // 2
B_HEADS = 4
B_V_DIM = B_WIDTH // B_HEADS
B_QK_DIM = B_V_DIM // 2
C_WIDTH = D_MODEL // 4
C_HEAD_DIM = 64
C_HEADS = C_WIDTH // C_HEAD_DIM
C_KV_HEADS = 2
C_GROUP = C_HEADS // C_KV_HEADS
WINDOW = 128
D_MIX = A_WIDTH + B_WIDTH + C_WIDTH

OFF_UV = 0
OFF_QB = OFF_UV + 2 * A_WIDTH
OFF_QC = OFF_QB + B_HEADS * 2 * B_QK_DIM
OFF_KB = OFF_QC + C_HEADS * C_HEAD_DIM
OFF_VB = OFF_KB + B_HEADS * 2 * B_QK_DIM
OFF_KC = OFF_VB + B_HEADS * B_V_DIM
OFF_VC = OFF_KC + C_KV_HEADS * C_HEAD_DIM
IN_COLS = OFF_VC + C_KV_HEADS * C_HEAD_DIM

T_ALL = SEQ + CTX_LEN
LANES = 128
TM = 256
N_TILES = T_ALL // TM
LATENT_TILES = SEQ // TM
CTX_TILE = N_TILES - 1
MOD_ROWS = 16
CTX_ROW = BATCH
MOD_TN = 1152
SUB_SLAB = 3
SUB_LATENT = 2
BAND = 2 * WINDOW + TM
N_BAND_OFFSETS = 3
LOG2E = math.log2(math.e)
ROW_BLK = 16
VMEM_LIMIT = 56 * 1024 * 1024

F32 = jnp.float32
BF16 = jnp.bfloat16


def _cparams(n_axes):
    return pltpu.CompilerParams(dimension_semantics=("arbitrary",) * n_axes, vmem_limit_bytes=VMEM_LIMIT)


def _rms(xf, g):
    ms = jnp.mean(xf * xf, axis=-1, keepdims=True)
    return xf * lax.rsqrt(ms + EPS) * g


def _dot(a, b):
    return jnp.dot(a, b, preferred_element_type=F32)


def _dot_nt(a, b):
    return lax.dot_general(a, b, (((1,), (1,)), ((), ())), preferred_element_type=F32)


def _resident(block, index):
    return pl.BlockSpec(block, lambda b, t: index, pipeline_mode=pl.Buffered(1))


def _mod_specs(layer):
    batch_row = pl.BlockSpec((None, None, N_MOD, D_MODEL), lambda b, t: (layer, b, 0, 0))
    ctx_row = pl.BlockSpec((None, None, N_MOD, D_MODEL), lambda b, t: (layer, CTX_ROW, 0, 0))
    return [batch_row, ctx_row]


def _mod_kernel(c_ref, w_ref, b_ref, o_ref):
    c = c_ref[...]
    sc = c * jax.nn.sigmoid(c)
    o_ref[...] = _dot(sc.astype(BF16), w_ref[...].astype(BF16)) + b_ref[...]


def _modulation(c_all, w_mod, b_mod):
    n_cols = N_MOD * D_MODEL
    return pl.pallas_call(
        _mod_kernel,
        grid=(DEPTH, n_cols // MOD_TN),
        in_specs=[
            pl.BlockSpec((MOD_ROWS, D_MODEL), lambda l, n: (0, 0)),
            pl.BlockSpec((None, D_MODEL, MOD_TN), lambda l, n: (l, 0, n)),
            pl.BlockSpec((None, 1, MOD_TN), lambda l, n: (l, 0, n)),
        ],
        out_specs=pl.BlockSpec((None, MOD_ROWS, MOD_TN), lambda l, n: (l, 0, n)),
        out_shape=jax.ShapeDtypeStruct((DEPTH, MOD_ROWS, n_cols), F32),
        compiler_params=_cparams(2),
        name="modulation",
    )(c_all, w_mod, b_mod.reshape(DEPTH, 1, n_cols))


def _ffn_kernel(*refs, j, n_sub, has_ctx, split_in):
    if split_in:
        x_refs, ctx_ref, rest = refs[:n_sub], refs[n_sub], refs[n_sub + 1:]
    else:
        s_ref, rest = refs[0], refs[1:]
    modb_ref, modc_ref, gpre_ref, gpost_ref, wg_ref, wu_ref, wd_ref, o_ref = rest
    last = pl.program_id(1) == pl.num_programs(1) - 1
    for i in range(n_sub):
        rows = slice(i * TM, (i + 1) * TM)
        ctx_rows = has_ctx and i == n_sub - 1
        if split_in:
            xf = x_refs[i][...]
            if ctx_rows:
                xf = jnp.where(last, ctx_ref[...], xf)
        else:
            xf = s_ref[rows, :]

        def mod_row(k):
            row = modb_ref[k:k + 1, :]
            return jnp.where(last, modc_ref[k:k + 1, :], row) if ctx_rows else row

        shift, scale, gate = mod_row(3 * j), mod_row(3 * j + 1), mod_row(3 * j + 2)
        y = (_rms(xf, gpre_ref[...] * (1.0 + scale)) + shift).astype(BF16)
        g = _dot(y, wg_ref[...])
        u = _dot(y, wu_ref[...])
        half = 0.5 * g
        h = ((half + half * jnp.tanh(half)) * u).astype(BF16)
        o = _dot(h, wd_ref[...])
        o_ref[rows, :] = xf + _rms(o, gpost_ref[...] * (FFN_RES * gate))


def _ffn(s, mod, layer, j, jj, norm_pre, norm_post, wg, wu, wd, ctx=None):
    split_in = ctx is not None
    rows = T_ALL if split_in else s.shape[1]
    has_ctx = rows == T_ALL
    n_sub = SUB_SLAB if has_ctx else SUB_LATENT
    tmf = n_sub * TM
    tile = pl.BlockSpec((None, tmf, D_MODEL), lambda b, t: (b, t, 0))
    if split_in:
        def latent_tile(i):
            return pl.BlockSpec((None, TM, D_MODEL),
                                lambda b, t: (b, jnp.minimum(n_sub * t + i, LATENT_TILES - 1), 0))
        data_specs = [latent_tile(i) for i in range(n_sub)]
        data_specs.append(pl.BlockSpec((None, TM, D_MODEL), lambda b, t: (b, 0, 0)))
        data = [s] * n_sub + [ctx]
    else:
        data_specs, data = [tile], [s]
    return pl.pallas_call(
        functools.partial(_ffn_kernel, j=j, n_sub=n_sub, has_ctx=has_ctx, split_in=split_in),
        grid=(BATCH, rows // tmf),
        in_specs=data_specs + _mod_specs(layer) + [
            _resident((None, None, 1, D_MODEL), (layer, j, 0, 0)),
            _resident((None, None, 1, D_MODEL), (layer, j, 0, 0)),
            _resident((None, None, D_MODEL, D_FF), (layer, jj, 0, 0)),
            _resident((None, None, D_MODEL, D_FF), (layer, jj, 0, 0)),
            _resident((None, None, D_FF, D_MODEL), (layer, jj, 0, 0)),
        ],
        out_specs=tile,
        out_shape=jax.ShapeDtypeStruct((BATCH, rows, D_MODEL), F32),
        compiler_params=_cparams(2),
        name="ffn",
    )(*data, mod, mod, norm_pre, norm_post, wg, wu, wd)


def _mix_in_kernel(s_ref, modb_ref, modc_ref, gpre_ref, win_ref, cos_ref, sina_ref, sinb_ref, vgain_ref, ws_ref,
                   bs_ref, seg_ref, oa_ref, qb_ref, qc_ref, kb_ref, vb_ref, kc_ref, vc_ref, *, n_sub):
    last = pl.program_id(1) == pl.num_programs(1) - 1
    quarter = C_HEAD_DIM // 4
    lane = lax.broadcasted_iota(jnp.int32, (TM, LANES), 1)
    low = lane < (LANES // 2)
    a_lane = lax.broadcasted_iota(jnp.int32, (CHUNK, A_WIDTH), 1)
    q_scale = B_QK_DIM ** -0.5 * LOG2E
    c_scale = C_HEAD_DIM ** -0.5 * LOG2E
    ones_blk = jnp.ones((TM, LANES), BF16)

    for i in range(n_sub):
        rows = slice(i * TM, (i + 1) * TM)
        ctx_rows = i == n_sub - 1

        def mod_row(k):
            row = modb_ref[k:k + 1, :]
            return jnp.where(last, modc_ref[k:k + 1, :], row) if ctx_rows else row

        ax = (_rms(s_ref[rows, :], gpre_ref[...] * (1.0 + mod_row(4))) + mod_row(3)).astype(BF16)
        proj = _dot(ax, win_ref[...])

        cos = cos_ref[rows, :]
        sin_a = sina_ref[rows, :]
        sin_b = sinb_ref[rows, :]

        def rope(xb):
            return xb * cos + pltpu.roll(xb, LANES - quarter, 1) * sin_a + pltpu.roll(xb, quarter, 1) * sin_b

        def block(off, n):
            return proj[:, off + n * LANES:off + (n + 1) * LANES]

        for h in range(B_HEADS):
            q = rope(block(OFF_QB, h)) * q_scale
            qb_ref[rows, (2 * h) * LANES:(2 * h + 1) * LANES] = jnp.where(low, q, 0.0).astype(BF16)
            qb_ref[rows, (2 * h + 1) * LANES:(2 * h + 2) * LANES] = jnp.where(low, 0.0, q).astype(BF16)
            kb_ref[rows, h * LANES:(h + 1) * LANES] = rope(block(OFF_KB, h)).astype(BF16)
            vb_ref[rows, (2 * h) * LANES:(2 * h + 1) * LANES] = block(OFF_VB, h).astype(BF16)
            vb_ref[rows, (2 * h + 1) * LANES:(2 * h + 2) * LANES] = ones_blk

        for kv in range(C_KV_HEADS):
            q = rope(block(OFF_QC, kv)) * c_scale
            q_sw = pltpu.roll(q, LANES // 2, 1)
            own_low = kv == 0
            g0 = jnp.where(low, q, 0.0) if own_low else jnp.where(low, 0.0, q_sw)
            g1 = jnp.where(low, q_sw, 0.0) if own_low else jnp.where(low, 0.0, q)
            qc_ref[rows, (2 * kv) * LANES:(2 * kv + 1) * LANES] = g0.astype(BF16)
            qc_ref[rows, (2 * kv + 1) * LANES:(2 * kv + 2) * LANES] = g1.astype(BF16)
        kc_ref[rows, :] = rope(block(OFF_KC, 0)).astype(BF16)
        vc_ref[rows, :] = proj[:, OFF_VC:OFF_VC + LANES].astype(BF16)

        uv = jax.nn.gelu(proj[:, OFF_UV:OFF_UV + 2 * A_WIDTH], approximate=True)
        u = uv[:, :A_WIDTH]
        v = uv[:, A_WIDTH:]
        v2 = v * v
        v2_hi = v2.astype(BF16)
        v2_lo = (v2 - v2_hi.astype(F32)).astype(BF16)
        seg = seg_ref[...]
        ms = (_dot(v2_hi, seg) + _dot(v2_lo, seg)) * (1.0 / A_HEAD_DIM)
        vn = (v * lax.rsqrt(ms + EPS) * vgain_ref[...]).astype(BF16)
        for c in range(TM // CHUNK):
            vc = vn[c * CHUNK:(c + 1) * CHUNK, :]
            mixed = _dot(ws_ref[A_HEADS - 1], vc)
            for h in range(A_HEADS - 2, -1, -1):
                mixed = jnp.where(a_lane < (h + 1) * A_HEAD_DIM, _dot(ws_ref[h], vc), mixed)
            mixed = mixed + bs_ref[...]
            out_rows = slice(i * TM + c * CHUNK, i * TM + (c + 1) * CHUNK)
            oa_ref[out_rows, :] = (u[c * CHUNK:(c + 1) * CHUNK, :] * mixed).astype(BF16)


def _mix_in(s, mod, layer, norm_pre, w_in, tables, v_gain, w_s, b_s, seg):
    cos_t, sin_a, sin_b = tables
    n_sub = SUB_SLAB
    tmi = n_sub * TM
    tok = lambda width: pl.BlockSpec((None, tmi, width), lambda b, t: (b, t, 0))
    pos = pl.BlockSpec((tmi, LANES), lambda b, t: (t, 0))
    out_widths = (A_WIDTH, 2 * B_WIDTH, 2 * C_WIDTH, B_WIDTH, 2 * B_WIDTH, LANES, LANES)
    return pl.pallas_call(
        functools.partial(_mix_in_kernel, n_sub=n_sub),
        grid=(BATCH, T_ALL // tmi),
        in_specs=[tok(D_MODEL)] + _mod_specs(layer) + [
            _resident((None, None, 1, D_MODEL), (layer, 1, 0, 0)),
            _resident((None, D_MODEL, IN_COLS), (layer, 0, 0)),
            pos, pos, pos,
            _resident((None, 1, A_WIDTH), (layer, 0, 0)),
            _resident((None, A_HEADS, CHUNK, CHUNK), (layer, 0, 0, 0)),
            _resident((None, CHUNK, A_WIDTH), (layer, 0, 0)),
            _resident((A_WIDTH, A_WIDTH), (0, 0)),
        ],
        out_specs=[tok(w) for w in out_widths],
        out_shape=[jax.ShapeDtypeStruct((BATCH, T_ALL, w), BF16) for w in out_widths],
        compiler_params=_cparams(2),
        name="mix_in",
    )(s, mod, mod, norm_pre, w_in, cos_t, sin_a, sin_b, v_gain, w_s, b_s, seg)


def _mix_attn_kernel(s_ref, oa_ref, qb_ref, qc_ref, kb_ref, vb_ref, kc_ref, vc_ref, mod_ref, gpost_ref,
                     subln_ref, lamp_ref, sink_ref, wout_ref, band_ref, o_ref, *scratch, n_sub, ctx_mode, lam_init):
    n_keys = CTX_LEN if ctx_mode else T_ALL
    s_scrs, p_scrs = scratch[:n_sub], scratch[n_sub:]
    t = pl.program_id(1)

    lam_p = lamp_ref[...]
    lam = (jnp.exp(jnp.sum(lam_p[0:1, :] * lam_p[1:2, :], axis=-1, keepdims=True))
           - jnp.exp(jnp.sum(lam_p[2:3, :] * lam_p[3:4, :], axis=-1, keepdims=True)) + lam_init)

    row2 = lax.broadcasted_iota(jnp.int32, (2 * TM, 1), 0)
    lane = lax.broadcasted_iota(jnp.int32, (TM, LANES), 1)
    low = lane < (LANES // 2)
    pair = 2 * LANES

    def tile_rows(i):
        return slice(i * TM, (i + 1) * TM)

    def swa_heads(i):
        rows = tile_rows(i)
        if ctx_mode:
            k_all, v_all = kc_ref[...], vc_ref[...]
        else:
            q_pos0 = (t * n_sub + i) * TM
            band_start = pl.multiple_of(jnp.clip(q_pos0 - WINDOW, 0, SEQ - BAND), LANES)
            k_all = jnp.concatenate([kc_ref[pl.ds(band_start, BAND), :], kc_ref[SEQ:T_ALL, :]], axis=0)
            v_all = jnp.concatenate([vc_ref[pl.ds(band_start, BAND), :], vc_ref[SEQ:T_ALL, :]], axis=0)
            bias = band_ref[(q_pos0 - band_start) // WINDOW]
        v_all = jnp.concatenate([v_all, jnp.ones_like(v_all)], axis=1)
        outs = []
        for kv in range(C_KV_HEADS):
            qq = jnp.concatenate([qc_ref[rows, (2 * kv) * LANES:(2 * kv + 1) * LANES],
                                  qc_ref[rows, (2 * kv + 1) * LANES:(2 * kv + 2) * LANES]], axis=0)
            sink = jnp.where(row2 < TM, sink_ref[0:1, 2 * kv:2 * kv + 1],
                             sink_ref[0:1, 2 * kv + 1:2 * kv + 2]) * LOG2E
            s = _dot_nt(qq, k_all)
            if not ctx_mode:
                s = jnp.concatenate([
                    jnp.concatenate([s[:TM, :BAND] + bias, s[:TM, BAND:]], axis=1),
                    jnp.concatenate([s[TM:, :BAND] + bias, s[TM:, BAND:]], axis=1)], axis=0)
            m = jnp.maximum(jnp.max(s, axis=-1, keepdims=True), sink)
            pv = _dot(jnp.exp2(s - m).astype(BF16), v_all)
            o = pv[:, :LANES] * (1.0 / (pv[:, LANES:] + jnp.exp2(sink - m)))
            o_g0, o_g1 = o[:TM], o[TM:]
            if kv == 0:
                blk = jnp.where(low, o_g0, pltpu.roll(o_g1, LANES // 2, 1))
            else:
                blk = jnp.where(low, pltpu.roll(o_g0, LANES // 2, 1), o_g1)
            outs.append(blk.astype(BF16))
        return outs

    row_max = [None] * n_sub

    def scores(i, h):
        rows = tile_rows(i)
        qq = jnp.concatenate([qb_ref[rows, (2 * h) * LANES:(2 * h + 1) * LANES],
                              qb_ref[rows, (2 * h + 1) * LANES:(2 * h + 2) * LANES]], axis=0)
        sv = _dot_nt(qq, kb_ref[:, h * LANES:(h + 1) * LANES])
        s_scrs[i][:, 0:n_keys] = sv
        row_max[i] = jnp.max(sv, axis=-1, keepdims=True)

    def diff_head(i, h):
        s_scr, p_scr = s_scrs[i], p_scrs[2 * i + h % 2]
        m = row_max[i]
        for rb in range(2 * TM // ROW_BLK):
            rows = slice(rb * ROW_BLK, (rb + 1) * ROW_BLK)
            m_blk = m[rows]
            for c in range(n_keys // LANES):
                cols = slice(c * LANES, (c + 1) * LANES)
                p_scr[rows, cols] = jnp.exp2(s_scr[rows, cols] - m_blk).astype(BF16)
        pv = _dot(p_scr[:, 0:n_keys], vb_ref[:, (2 * h) * LANES:(2 * h + 2) * LANES])
        num, r = pv[:, 0:LANES], 1.0 / pv[:, LANES:2 * LANES]
        o = num[:TM] * r[:TM] - num[TM:] * (lam * r[TM:])
        return _rms(o, subln_ref[...] * (1.0 - lam_init)).astype(BF16)

    mix = []
    for i in range(n_sub):
        oc = swa_heads(i)
        part = _dot(oa_ref[tile_rows(i), :], wout_ref[0:A_WIDTH, :])
        mix.append(part + _dot(jnp.concatenate(oc, axis=-1), wout_ref[A_WIDTH + B_WIDTH:D_MIX, :]))
    for i in range(n_sub):
        scores(i, 0)
    ob = [[] for _ in range(n_sub)]
    for h in range(B_HEADS):
        for i in range(n_sub):
            ob[i].append(diff_head(i, h))
            if h + 1 < B_HEADS:
                scores(i, h + 1)
            if h % 2 == 1:
                w_rows = slice(A_WIDTH + (h // 2) * pair, A_WIDTH + (h // 2 + 1) * pair)
                mix[i] = mix[i] + _dot(jnp.concatenate(ob[i][h - 1:h + 1], axis=-1), wout_ref[w_rows, :])
    gate = mod_ref[5:6, :]
    for i in range(n_sub):
        rows = tile_rows(i)
        o_ref[rows, :] = s_ref[rows, :] + _rms(mix[i], gpost_ref[...] * gate)


def _mix_attn(s, oa, qb, qc, kb, vb, kc, vc, mod, layer, norm_post, subln, lam_p, sink, w_out, band_bias, *,
              ctx_mode, lam_init):
    n_sub = 1 if ctx_mode else SUB_LATENT
    tma = n_sub * TM
    out_rows = CTX_LEN if ctx_mode else SEQ
    n_keys = CTX_LEN if ctx_mode else T_ALL
    t0 = CTX_TILE if ctx_mode else 0
    tok = lambda width: pl.BlockSpec((None, tma, width), lambda b, t: (b, t + t0, 0))
    if ctx_mode:
        keys = lambda width: pl.BlockSpec((None, CTX_LEN, width), lambda b, t: (b, CTX_TILE, 0))
        mod_spec = pl.BlockSpec((None, None, N_MOD, D_MODEL), lambda b, t: (layer, CTX_ROW, 0, 0))
    else:
        keys = lambda width: pl.BlockSpec((None, T_ALL, width), lambda b, t: (b, 0, 0))
        mod_spec = pl.BlockSpec((None, None, N_MOD, D_MODEL), lambda b, t: (layer, b, 0, 0))
    return pl.pallas_call(
        functools.partial(_mix_attn_kernel, n_sub=n_sub, ctx_mode=ctx_mode, lam_init=lam_init),
        grid=(BATCH, out_rows // tma),
        in_specs=[
            tok(D_MODEL), tok(A_WIDTH), tok(2 * B_WIDTH), tok(2 * C_WIDTH),
            keys(B_WIDTH), keys(2 * B_WIDTH), keys(LANES), keys(LANES),
            mod_spec,
            _resident((None, None, 1, D_MODEL), (layer, 1, 0, 0)),
            _resident((None, 1, B_V_DIM), (layer, 0, 0)),
            _resident((None, 4, B_QK_DIM), (layer, 0, 0)),
            _resident((None, 1, C_HEADS), (layer, 0, 0)),
            _resident((None, D_MIX, D_MODEL), (layer, 0, 0)),
            _resident((N_BAND_OFFSETS, TM, BAND), (0, 0, 0)),
        ],
        out_specs=pl.BlockSpec((None, tma, D_MODEL), lambda b, t: (b, t, 0)),
        out_shape=jax.ShapeDtypeStruct((BATCH, out_rows, D_MODEL), F32),
        scratch_shapes=([pltpu.VMEM((2 * TM, n_keys), F32)] * n_sub
                        + [pltpu.VMEM((2 * TM, n_keys), BF16)] * (2 * n_sub)),
        compiler_params=_cparams(2),
        name="mix_attn_ctx" if ctx_mode else "mix_attn",
    )(s, oa, qb, qc, kb, vb, kc, vc, mod, norm_post, subln, lam_p, sink, w_out, band_bias)


def _rope_tables():
    rows = SEQ // GRID_W
    row = jnp.repeat(jnp.arange(rows, dtype=F32), GRID_W)
    col = jnp.tile(jnp.arange(GRID_W, dtype=F32), rows)
    quarter = C_HEAD_DIM // 4
    inv = ROPE_BASE ** (-jnp.arange(quarter, dtype=F32) / quarter)
    ar = row[:, None] * inv[None, :]
    ac = col[:, None] * inv[None, :]
    ang = jnp.concatenate([ar, ar, ac, ac] * (LANES // C_HEAD_DIM), axis=-1)
    cos = jnp.concatenate([jnp.cos(ang), jnp.ones((CTX_LEN, LANES), F32)], axis=0)
    sin = jnp.concatenate([jnp.sin(ang), jnp.zeros((CTX_LEN, LANES), F32)], axis=0)
    first = (jnp.arange(LANES) // quarter) % 2 == 0
    sin_a = jnp.where(first[None, :], -sin, 0.0)
    sin_b = jnp.where(first[None, :], 0.0, sin)
    return cos, sin_a, sin_b


def kernel(x, c, ctx, c_ctx, w_mod, b_mod, norm_pre, norm_post, ffn_w_gate, ffn_w_up, ffn_w_down, w_in, w_out,
           gmlp_v_gain, gmlp_w_s, gmlp_b_s, diff_lambda, diff_subln, swa_sink):
    assert x.shape == (BATCH, SEQ, D_MODEL) and ctx.shape == (BATCH, CTX_LEN, D_MODEL)
    c_all = jnp.zeros((MOD_ROWS, D_MODEL), F32).at[:BATCH].set(c).at[CTX_ROW].set(c_ctx)
    mod = _modulation(c_all, w_mod, b_mod).reshape(DEPTH, MOD_ROWS, N_MOD, D_MODEL)

    tables = _rope_tables()
    q_in_band = jnp.arange(TM)[None, :, None] + WINDOW * jnp.arange(N_BAND_OFFSETS)[:, None, None]
    band_bias = jnp.where(jnp.abs(q_in_band - jnp.arange(BAND)[None, None, :]) <= WINDOW, 0.0, NEG_INF).astype(F32)
    head_of = jnp.arange(A_WIDTH) // A_HEAD_DIM
    seg = (head_of[:, None] == head_of[None, :]).astype(BF16)
    wg, wu, wd = ffn_w_gate.astype(BF16), ffn_w_up.astype(BF16), ffn_w_down.astype(BF16)
    w_in_b, w_out_b, w_s_b = w_in.astype(BF16), w_out.astype(BF16), gmlp_w_s.astype(BF16)
    b_s = jnp.repeat(jnp.swapaxes(gmlp_b_s, 1, 2), A_HEAD_DIM, axis=2)
    n_pre = norm_pre.reshape(DEPTH, 3, 1, D_MODEL)
    n_post = norm_post.reshape(DEPTH, 3, 1, D_MODEL)
    v_gain = gmlp_v_gain.reshape(DEPTH, 1, A_WIDTH)
    subln = diff_subln.reshape(DEPTH, 1, B_V_DIM)
    sink = swa_sink.reshape(DEPTH, 1, C_HEADS)

    s = _ffn(x, mod, 0, 0, 0, n_pre, n_post, wg, wu, wd, ctx=ctx)
    for l in range(DEPTH):
        lam_init = 0.8 - 0.6 * math.exp(-0.3 * l)
        if l > 0:
            s = _ffn(s, mod, l, 0, 0, n_pre, n_post, wg, wu, wd)
        proj = _mix_in(s, mod, l, n_pre, w_in_b, tables, v_gain, w_s_b, b_s, seg)
        attn = functools.partial(_mix_attn, s, *proj, mod, l, n_post, subln, diff_lambda, sink, w_out_b,
                                 band_bias, lam_init=lam_init)
        latent = attn(ctx_mode=False)
        if l == DEPTH - 1:
            return _ffn(latent, mod, l, 2, 1, n_pre, n_post, wg, wu, wd)
        s = _ffn(latent, mod, l, 2, 1, n_pre, n_post, wg, wu, wd, ctx=attn(ctx_mode=True))
```
